```python
import math
import jax, jax.numpy as jnp
from jax import lax
import numpy as np

D_MODEL = 4096
BATCH = 2
SEQ = 8192
DEPTH = 4

HEAD_DIM = 128
N_HEADS = D_MODEL // HEAD_DIM
HEADS_NA = N_HEADS // 4
HEADS_DIL = 3 * N_HEADS // 8
HEADS_GQA = N_HEADS - HEADS_NA - HEADS_DIL
KV_HEADS_GQA = HEADS_GQA // 3
MIX_WIDTH = N_HEADS * HEAD_DIM
KV_WIDTH = (HEADS_NA + HEADS_DIL + KV_HEADS_GQA) * HEAD_DIM
D_FF = 4 * D_MODEL
GRID_W = 64
NA_ROWS = 8
NA_COLS = 16
DIL_PATTERNS = ((128, 1), (512, 4), (2048, 16))
T5_BUCKETS = 32
T5_MAX_DIST = 1024
ROPE_THETA = 10000.0
QUERY_BLOCK = 128
N_MOD = 6
EPS = 1e-6
NEG_INF = -1e30
ATTN_SCALE = HEAD_DIM ** -0.5

kernel_name = "hybrid_parallel_heads_bidir_encoder"


def rms_norm(x, g):
    xf = x.astype(jnp.float32)
    y = xf * lax.rsqrt(jnp.mean(xf * xf, axis=-1, keepdims=True) + EPS)
    return (y * g.astype(jnp.float32)).astype(x.dtype)


def modulate(h, shift, scale):
    return h * (1.0 + scale[:, None, :]) + shift[:, None, :]


def t5_bucket(rel):
    nb = T5_BUCKETS // 2
    max_exact = nb // 2
    ret = jnp.where(rel > 0, nb, 0)
    n = jnp.abs(rel)
    nf = jnp.maximum(n, 1).astype(jnp.float32)
    large = max_exact + (jnp.log(nf / max_exact) / math.log(T5_MAX_DIST / max_exact)
                         * (nb - max_exact)).astype(jnp.int32)
    large = jnp.minimum(large, nb - 1)
    return ret + jnp.where(n < max_exact, n, large)


def axial_rope(x):
    S = x.shape[1]
    t = jnp.arange(S)
    half = HEAD_DIM // 2
    inv = jnp.exp(-math.log(ROPE_THETA) * jnp.arange(0, half, 2, dtype=jnp.float32) / half)

    def rot(xa, pos):
        ang = pos.astype(jnp.float32)[:, None] * inv[None, :]
        cos = jnp.cos(ang)[None, :, None, :]
        sin = jnp.sin(ang)[None, :, None, :]
        x1, x2 = jnp.split(xa.astype(jnp.float32), 2, axis=-1)
        return jnp.concatenate([x1 * cos - x2 * sin, x1 * sin + x2 * cos], axis=-1)

    out = jnp.concatenate([rot(x[..., :half], t // GRID_W), rot(x[..., half:], t % GRID_W)], axis=-1)
    return out.astype(x.dtype)


def neighborhood_attention(q, k, v, rpb):
    Bn, H, S, hd = q.shape
    rows = S // GRID_W
    kr = min(NA_ROWS, rows)
    qg = q.reshape(Bn, H, rows, GRID_W, hd)
    kg = k.reshape(Bn, H, rows, GRID_W, hd)
    vg = v.reshape(Bn, H, rows, GRID_W, hd)
    col = jnp.arange(GRID_W)
    cs = jnp.clip(col - NA_COLS // 2, 0, GRID_W - NA_COLS)
    cidx = cs[:, None] + jnp.arange(NA_COLS)[None, :]
    coff = cidx - col[:, None] + NA_COLS - 1

    def row_fn(args):
        q_row, r = args
        rs = jnp.clip(r - kr // 2, 0, rows - kr)
        k_rows = lax.dynamic_slice_in_dim(kg, rs, kr, axis=2)
        v_rows = lax.dynamic_slice_in_dim(vg, rs, kr, axis=2)
        k_nb = k_rows[:, :, :, cidx, :]
        v_nb = v_rows[:, :, :, cidx, :]
        roff = rs + jnp.arange(kr) - r + NA_ROWS - 1
        bias = rpb[:, roff[None, :, None], coff[:, None, :]]
        s = jnp.einsum('bhqd,bhrqcd->bhqrc', q_row, k_nb).astype(jnp.float32) + bias.astype(jnp.float32)
        p = jax.nn.softmax(s.reshape(Bn, H, GRID_W, kr * NA_COLS), axis=-1).reshape(s.shape)
        return jnp.einsum('bhqrc,bhrqcd->bhqd', p.astype(v.dtype), v_nb)

    o = lax.map(row_fn, (qg.transpose(2, 0, 1, 3, 4), jnp.arange(rows)))
    return o.transpose(1, 2, 0, 3, 4).reshape(Bn, H, S, hd)


def dilated_window_attention(q, k, v, t5_table, d, half):
    Bn, H, S, hd = q.shape
    L = S // d
    qb = math.gcd(L, QUERY_BLOCK)
    nb = L // qb
    win = qb + 2 * half

    def regroup(a):
        return a.reshape(Bn, H, L, d, hd).transpose(0, 1, 3, 2, 4)

    qr = regroup(q).reshape(Bn, H, d, nb, qb, hd)
    pad_cfg = ((0, 0), (0, 0), (0, 0), (half, half), (0, 0))
    kp = jnp.pad(regroup(k), pad_cfg)
    vp = jnp.pad(regroup(v), pad_cfg)
    idx = (jnp.arange(nb) * qb)[:, None] + jnp.arange(win)[None, :]
    kw = kp[:, :, :, idx, :]
    vw = vp[:, :, :, idx, :]
    j = jnp.arange(win)[None, :] - half - jnp.arange(qb)[:, None]
    key_pos = idx - half
    valid = ((jnp.abs(j) <= half)[None]
             & (key_pos >= 0)[:, None, :] & (key_pos < L)[:, None, :])
    bias = t5_table[t5_bucket(j * d)].transpose(2, 0, 1).astype(jnp.float32)
    s = jnp.einsum('bhrnqd,bhrnkd->bhrnqk', qr, kw).astype(jnp.float32) + bias[None, :, None, None]
    s = jnp.where(valid[None, None, None], s, NEG_INF)
    m = jnp.max(s, axis=-1, keepdims=True)
    p = jnp.exp(s - m)
    den = jnp.sum(p, axis=-1, keepdims=True)
    o = jnp.einsum('bhrnqk,bhrnkd->bhrnqd', p, vw.astype(jnp.float32)) / den
    lse = (m + jnp.log(den))[..., 0]
    o = o.reshape(Bn, H, d, L, hd).transpose(0, 1, 3, 2, 4).reshape(Bn, H, S, hd)
    lse = lse.reshape(Bn, H, d, L).transpose(0, 1, 3, 2).reshape(Bn, H, S)
    return o, lse


def dilated_mixture_attention(q, k, v, t5_table):
    outs, lses = [], []
    for window, d in DIL_PATTERNS:
        o, lse = dilated_window_attention(q, k, v, t5_table, d, window // (2 * d))
        outs.append(o)
        lses.append(lse)
    w = jax.nn.softmax(jnp.stack(lses, axis=0), axis=0)
    return jnp.sum(w[..., None] * jnp.stack(outs, axis=0), axis=0)


def gqa_attention(q, k, v):
    Bn, H, S, hd = q.shape
    kvh = k.shape[1]
    grp = H // kvh
    nblk = S // QUERY_BLOCK
    qblocks = q.reshape(Bn, kvh, grp, nblk, QUERY_BLOCK, hd).transpose(3, 0, 1, 2, 4, 5)

    def block(q_blk):
        s = jnp.einsum('bkgqd,bksd->bkgqs', q_blk, k).astype(jnp.float32)
        p = jax.nn.softmax(s, axis=-1)
        return jnp.einsum('bkgqs,bksd->bkgqd', p.astype(v.dtype), v)

    o = lax.map(block, qblocks)
    return o.transpose(1, 2, 3, 0, 4, 5).reshape(Bn, H, S, hd)


def token_mixer(h, w_in, w_out, q_gain, k_gain, na_rpb, t5_table):
    Bn, S, _ = h.shape
    qkv = h @ w_in
    q, k, v = jnp.split(qkv, [MIX_WIDTH, MIX_WIDTH + KV_WIDTH], axis=-1)
    cut = [HEADS_NA * HEAD_DIM, (HEADS_NA + HEADS_DIL) * HEAD_DIM]
    q_na, q_dil, q_gqa = jnp.split(q, cut, axis=-1)
    k_na, k_dil, k_gqa = jnp.split(k, cut, axis=-1)
    v_na, v_dil, v_gqa = jnp.split(v, cut, axis=-1)

    def heads(a):
        return a.reshape(Bn, S, -1, HEAD_DIM).transpose(0, 2, 1, 3)

    o_na = neighborhood_attention(heads(q_na * ATTN_SCALE), heads(k_na), heads(v_na), na_rpb)
    o_dil = dilated_mixture_attention(heads(q_dil * ATTN_SCALE), heads(k_dil), heads(v_dil), t5_table)
    qg = axial_rope(rms_norm(q_gqa.reshape(Bn, S, HEADS_GQA, HEAD_DIM), q_gain)) * ATTN_SCALE
    kg = axial_rope(rms_norm(k_gqa.reshape(Bn, S, KV_HEADS_GQA, HEAD_DIM), k_gain))
    o_gqa = gqa_attention(qg.transpose(0, 2, 1, 3), kg.transpose(0, 2, 1, 3), heads(v_gqa))

    def merge(o):
        return o.transpose(0, 2, 1, 3).reshape(Bn, S, -1).astype(h.dtype)

    o = jnp.concatenate([merge(o_na), merge(o_dil), merge(o_gqa)], axis=-1)
    return o @ w_out


def squared_relu_mlp(h, w1, w2):
    return jnp.square(jax.nn.relu(h @ w1)) @ w2


def setup_inputs(seed: int = 0) -> dict:
    key = jax.random.key(seed)
    ks = jax.random.split(key, 14)
    f32 = jnp.float32
    nrm = jax.random.normal
    x = nrm(ks[0], (BATCH, SEQ, D_MODEL), f32)
    c = nrm(ks[1], (BATCH, D_MODEL), f32)
    ada_w = nrm(ks[2], (D_MODEL, N_MOD * D_MODEL), f32) * D_MODEL ** -0.5
    ada_b = 0.01 * nrm(ks[3], (N_MOD * D_MODEL,), f32)
    ada_layer_emb = 0.1 * nrm(ks[4], (DEPTH, N_MOD, D_MODEL), f32)
    norm_gains = 1.0 + 0.05 * nrm(ks[5], (DEPTH, 4, D_MODEL), f32)
    w_in = nrm(ks[6], (DEPTH, D_MODEL, MIX_WIDTH + 2 * KV_WIDTH), f32) * D_MODEL ** -0.5
    w_out = nrm(ks[7], (DEPTH, MIX_WIDTH, D_MODEL), f32) * MIX_WIDTH ** -0.5
    q_gain = 1.0 + 0.05 * nrm(ks[8], (DEPTH, HEAD_DIM), f32)
    k_gain = 1.0 + 0.05 * nrm(ks[9], (DEPTH, HEAD_DIM), f32)
    na_rpb = 0.2 * nrm(ks[10], (DEPTH, HEADS_NA, 2 * NA_ROWS - 1, 2 * NA_COLS - 1), f32)
    t5_table = 0.2 * nrm(ks[11], (T5_BUCKETS, HEADS_DIL), f32)
    w_mlp_in = nrm(ks[12], (DEPTH, D_MODEL, D_FF), f32) * D_MODEL ** -0.5
    w_mlp_out = nrm(ks[13], (DEPTH, D_FF, D_MODEL), f32) * D_FF ** -0.5
    return {"x": x, "c": c, "ada_w": ada_w, "ada_b": ada_b, "ada_layer_emb": ada_layer_emb,
            "norm_gains": norm_gains, "w_in": w_in, "w_out": w_out, "q_gain": q_gain,
            "k_gain": k_gain, "na_rpb": na_rpb, "t5_table": t5_table,
            "w_mlp_in": w_mlp_in, "w_mlp_out": w_mlp_out}


def reference(x, c, ada_w, ada_b, ada_layer_emb, norm_gains, w_in, w_out, q_gain, k_gain,
              na_rpb, t5_table, w_mlp_in, w_mlp_out):
    mod = (jax.nn.silu(c) @ ada_w + ada_b).reshape(c.shape[0], N_MOD, D_MODEL)
    for l in range(DEPTH):
        m = mod + ada_layer_emb[l][None]
        sh1, sc1, g1, sh2, sc2, g2 = m[:, 0], m[:, 1], m[:, 2], m[:, 3], m[:, 4], m[:, 5]
        h = modulate(rms_norm(x, norm_gains[l, 0]), sh1, sc1)
        y = token_mixer(h, w_in[l], w_out[l], q_gain[l], k_gain[l], na_rpb[l], t5_table)
        x = x + g1[:, None, :] * rms_norm(y, norm_gains[l, 1])
        h = modulate(rms_norm(x, norm_gains[l, 2]), sh2, sc2)
        y = squared_relu_mlp(h, w_mlp_in[l], w_mlp_out[l])
        x = x + g2[:, None, :] * rms_norm(y, norm_gains[l, 3])
    return x
```

```python
import functools
import math

import numpy as np
import jax
import jax.numpy as jnp
from jax import lax
from jax.experimental import pallas as pl
from jax.experimental.pallas import tpu as pltpu

HEAD_DIM = 128
GRID_W = 64
NA_ROWS = 8
NA_COLS = 16
DIL_PATTERNS = ((128, 1), (512, 4), (2048, 16))
T5_BUCKETS = 32
T5_MAX_DIST = 1024
ROPE_THETA = 10000.0
N_MOD = 6
EPS = 1e-6
NEG_INF = -1e30
ATTN_SCALE = HEAD_DIM ** -0.5

V7X_VMEM_BYTES = 64 * 1024 * 1024
V7X_LANES = 128

BF16 = jnp.bfloat16
F32 = jnp.float32


def _params(semantics, vmem_bytes):
    limit = min(int(vmem_bytes * 1.25) + (4 << 20), V7X_VMEM_BYTES - (6 << 20))
    return pltpu.CompilerParams(dimension_semantics=semantics, vmem_limit_bytes=limit)


def _ada_kernel(c_ref, w_ref, b_ref, o_ref):
    c = c_ref[...]
    a = c * jax.nn.sigmoid(c)
    a_hi = a.astype(BF16)
    a_lo = (a - a_hi.astype(F32)).astype(BF16)
    w = w_ref[...]
    w_hi = w.astype(BF16)
    w_lo = (w - w_hi.astype(F32)).astype(BF16)
    acc = jnp.dot(a_hi, w_hi, preferred_element_type=F32)
    acc += jnp.dot(a_hi, w_lo, preferred_element_type=F32)
    acc += jnp.dot(a_lo, w_hi, preferred_element_type=F32)
    o_ref[...] = acc + b_ref[...]


def _ada_mod(c, ada_w, ada_b):
    bsz, d = c.shape
    n = ada_w.shape[1]
    rows = 8
    tn = 512
    c_pad = jnp.zeros((rows, d), F32).at[:bsz].set(c)
    out = pl.pallas_call(
        _ada_kernel,
        out_shape=jax.ShapeDtypeStruct((rows, n), F32),
        grid=(n // tn,),
        in_specs=[
            pl.BlockSpec((rows, d), lambda j: (0, 0)),
            pl.BlockSpec((d, tn), lambda j: (0, j)),
            pl.BlockSpec((1, tn), lambda j: (0, j)),
        ],
        out_specs=pl.BlockSpec((rows, tn), lambda j: (0, j)),
        compiler_params=_params(("arbitrary",), 2 * d * tn * 4 + 3 * d * tn * 2),
        name="ada_mod",
    )(c_pad, ada_w, ada_b.reshape(1, n))
    return out[:bsz]


def _norm_mod_kernel(x_ref, g_ref, m_ref, o_ref, *, shift_row, scale_row):
    x = x_ref[...]
    y = x * lax.rsqrt(jnp.mean(x * x, axis=-1, keepdims=True) + EPS) * g_ref[...]
    shift = m_ref[0, shift_row:shift_row + 1, :]
    scale = m_ref[0, scale_row:scale_row + 1, :]
    o_ref[...] = (y * (1.0 + scale) + shift).astype(o_ref.dtype)


def _norm_mod(x2, gain, m, shift_row, scale_row, seq):
    rows, d = x2.shape
    tr = 256
    per_b = seq // tr
    return pl.pallas_call(
        functools.partial(_norm_mod_kernel, shift_row=shift_row, scale_row=scale_row),
        out_shape=jax.ShapeDtypeStruct((rows, d), BF16),
        grid=(rows // tr,),
        in_specs=[
            pl.BlockSpec((tr, d), lambda i: (i, 0)),
            pl.BlockSpec((1, d), lambda i: (0, 0)),
            pl.BlockSpec((1, N_MOD, d), lambda i: (i // per_b, 0, 0)),
        ],
        out_specs=pl.BlockSpec((tr, d), lambda i: (i, 0)),
        compiler_params=_params(("parallel",), 2 * tr * d * 6 + 4 * tr * d * 4),
        name="norm_mod",
    )(x2, gain.reshape(1, d), m)


def _resid_norm_kernel(x_ref, y_ref, g_ref, m_ref, o_ref, *, gate_row):
    y = y_ref[...]
    yn = y * lax.rsqrt(jnp.mean(y * y, axis=-1, keepdims=True) + EPS) * g_ref[...]
    gate = m_ref[0, gate_row:gate_row + 1, :]
    o_ref[...] = x_ref[...] + gate * yn


def _resid_norm(x2, y2, gain, m, gate_row, seq):
    rows, d = x2.shape
    tr = 256
    per_b = seq // tr
    return pl.pallas_call(
        functools.partial(_resid_norm_kernel, gate_row=gate_row),
        out_shape=jax.ShapeDtypeStruct((rows, d), F32),
        grid=(rows // tr,),
        in_specs=[
            pl.BlockSpec((tr, d), lambda i: (i, 0)),
            pl.BlockSpec((tr, d), lambda i: (i, 0)),
            pl.BlockSpec((1, d), lambda i: (0, 0)),
            pl.BlockSpec((1, N_MOD, d), lambda i: (i // per_b, 0, 0)),
        ],
        out_specs=pl.BlockSpec((tr, d), lambda i: (i, 0)),
        compiler_params=_params(("parallel",), 2 * tr * d * 12 + 4 * tr * d * 4),
        name="resid_norm",
    )(x2, y2, gain.reshape(1, d), m)


def _swap_half_pairs(y):
    lane = lax.broadcasted_iota(jnp.int32, y.shape, 1)
    fwd = pltpu.roll(y, HEAD_DIM - 32, axis=1)
    bwd = pltpu.roll(y, 32, axis=1)
    return jnp.where((lane % 64) < 32, fwd, bwd)


def _qkv_kernel(a_ref, w_ref, cos_ref, sin_ref, qg_ref, kg_ref, o_ref, *,
                n_q_blocks, gq_lo, gq_hi, gk_lo, gk_hi, heads_per_block):
    j = pl.program_id(1)
    acc = jnp.dot(a_ref[...], w_ref[...], preferred_element_type=F32)
    is_q = j < n_q_blocks
    is_gq = (j >= gq_lo) & (j < gq_hi)
    is_gk = (j >= gk_lo) & (j < gk_hi)
    rope = is_gq | is_gk
    out_scale = jnp.where(is_q, ATTN_SCALE, 1.0).astype(F32)

    @pl.when(rope)
    def _():
        gain = jnp.where(is_gq, qg_ref[...], kg_ref[...])
        cos = cos_ref[...]
        sin = sin_ref[...]
        for h in range(heads_per_block):
            xh = acc[:, h * HEAD_DIM:(h + 1) * HEAD_DIM]
            y = xh * lax.rsqrt(jnp.mean(xh * xh, axis=-1, keepdims=True) + EPS) * gain
            y = y * cos + _swap_half_pairs(y) * sin
            o_ref[:, h * HEAD_DIM:(h + 1) * HEAD_DIM] = (y * out_scale).astype(o_ref.dtype)

    @pl.when(jnp.logical_not(rope))
    def _():
        o_ref[...] = (acc * out_scale).astype(o_ref.dtype)


def _qkv_proj(h2, w, cos_t, sin_t, q_gain, k_gain, cfg, seq):
    m, k = h2.shape
    n = w.shape[1]
    tm, tn = 1024, 512
    hpb = tn // HEAD_DIM
    blk = lambda head: head // hpb
    kern = functools.partial(
        _qkv_kernel,
        n_q_blocks=blk(cfg["n_heads"]),
        gq_lo=blk(cfg["q_gqa0"]), gq_hi=blk(cfg["n_heads"]),
        gk_lo=blk(cfg["k_gqa0"]), gk_hi=blk(cfg["v0"]),
        heads_per_block=hpb)
    per_b = seq // tm
    return pl.pallas_call(
        kern,
        out_shape=jax.ShapeDtypeStruct((m, n), BF16),
        grid=(m // tm, n // tn),
        in_specs=[
            pl.BlockSpec((tm, k), lambda i, j: (i, 0)),
            pl.BlockSpec((k, tn), lambda i, j: (0, j)),
            pl.BlockSpec((tm, HEAD_DIM), lambda i, j: (i % per_b, 0)),
            pl.BlockSpec((tm, HEAD_DIM), lambda i, j: (i % per_b, 0)),
            pl.BlockSpec((1, HEAD_DIM), lambda i, j: (0, 0)),
            pl.BlockSpec((1, HEAD_DIM), lambda i, j: (0, 0)),
        ],
        out_specs=pl.BlockSpec((tm, tn), lambda i, j: (i, j)),
        compiler_params=_params(
            ("parallel", "arbitrary"),
            2 * (tm * k * 2 + k * tn * 2 + tm * tn * 2 + 2 * tm * HEAD_DIM * 4) + 3 * tm * tn * 4),
        name="qkv_proj",
    )(h2, w, cos_t, sin_t, q_gain.reshape(1, HEAD_DIM), k_gain.reshape(1, HEAD_DIM))


def _out_proj_kernel(a0_ref, a1_ref, a2_ref, w_ref, o_ref, *, splits):
    s0, s1, s2 = splits
    acc = jnp.dot(a0_ref[...], w_ref[0:s0, :], preferred_element_type=F32)
    acc += jnp.dot(a1_ref[...], w_ref[s0:s0 + s1, :], preferred_element_type=F32)
    acc += jnp.dot(a2_ref[...], w_ref[s0 + s1:s0 + s1 + s2, :], preferred_element_type=F32)
    o_ref[...] = acc


def _out_proj(o_na, o_dil, o_gqa, w):
    m = o_na.shape[0]
    k, n = w.shape
    splits = (o_na.shape[1], o_dil.shape[1], o_gqa.shape[1])
    tm, tn = 1024, 1024
    return pl.pallas_call(
        functools.partial(_out_proj_kernel, splits=splits),
        out_shape=jax.ShapeDtypeStruct((m, n), F32),
        grid=(m // tm, n // tn),
        in_specs=[
            pl.BlockSpec((tm, splits[0]), lambda i, j: (i, 0)),
            pl.BlockSpec((tm, splits[1]), lambda i, j: (i, 0)),
            pl.BlockSpec((tm, splits[2]), lambda i, j: (i, 0)),
            pl.BlockSpec((k, tn), lambda i, j: (0, j)),
        ],
        out_specs=pl.BlockSpec((tm, tn), lambda i, j: (i, j)),
        compiler_params=_params(
            ("parallel", "arbitrary"),
            2 * (tm * k * 2 + k * tn * 2 + tm * tn * 4) + 2 * tm * tn * 4),
        name="out_proj",
    )(o_na, o_dil, o_gqa, w)


def _mlp_in_kernel(a_ref, w_ref, o_ref):
    acc = jnp.dot(a_ref[...], w_ref[...], preferred_element_type=F32)
    r = jnp.maximum(acc, 0.0)
    o_ref[...] = (r * r).astype(o_ref.dtype)


def _mlp_in(h2, w):
    m, k = h2.shape
    n = w.shape[1]
    tm, tn = 1024, 1024
    return pl.pallas_call(
        _mlp_in_kernel,
        out_shape=jax.ShapeDtypeStruct((m, n), BF16),
        grid=(m // tm, n // tn),
        in_specs=[
            pl.BlockSpec((tm, k), lambda i, j: (i, 0)),
            pl.BlockSpec((k, tn), lambda i, j: (0, j)),
        ],
        out_specs=pl.BlockSpec((tm, tn), lambda i, j: (i, j)),
        compiler_params=_params(
            ("parallel", "arbitrary"),
            2 * (tm * k * 2 + k * tn * 2 + tm * tn * 2) + 2 * tm * tn * 4),
        name="mlp_in",
    )(h2, w)


def _mlp_out_kernel(a_ref, w_ref, o_ref):
    kk = pl.program_id(2)
    acc = jnp.dot(a_ref[...], w_ref[...], preferred_element_type=F32)

    @pl.when(kk == 0)
    def _():
        o_ref[...] = acc

    @pl.when(kk != 0)
    def _():
        o_ref[...] += acc


def _mlp_out(u2, w):
    m, k = u2.shape
    n = w.shape[1]
    tm, tn, tk = 1024, 1024, 4096
    return pl.pallas_call(
        _mlp_out_kernel,
        out_shape=jax.ShapeDtypeStruct((m, n), F32),
        grid=(m // tm, n // tn, k // tk),
        in_specs=[
            pl.BlockSpec((tm, tk), lambda i, j, kk: (i, kk)),
            pl.BlockSpec((tk, tn), lambda i, j, kk: (kk, j)),
        ],
        out_specs=pl.BlockSpec((tm, tn), lambda i, j, kk: (i, j)),
        compiler_params=_params(
            ("parallel", "arbitrary", "arbitrary"),
            2 * (tm * tk * 2 + tk * tn * 2 + tm * tn * 4) + 2 * tm * tn * 4),
        name="mlp_out",
    )(u2, w)


def _qk_scores(q, k):
    return lax.dot_general(q, k, (((1,), (1,)), ((), ())), preferred_element_type=F32)


def _na_kernel(q_ref, k_ref, v_ref, b_ref, o_ref, *, grid_rows, q_rows, k_rows):
    i = pl.program_id(2)
    kb = jnp.clip(i * q_rows - NA_ROWS // 2, 0, grid_rows - k_rows) * GRID_W
    kb = pl.multiple_of(kb, 256)
    q = q_ref[...]
    kw = k_ref[pl.ds(kb, k_rows * GRID_W), :]
    vw = v_ref[pl.ds(kb, k_rows * GRID_W), :]
    s = _qk_scores(q, kw) + b_ref[0, 0]
    m = jnp.max(s, axis=-1, keepdims=True)
    p = jnp.exp(s - m)
    l = jnp.sum(p, axis=-1, keepdims=True)
    o = jnp.dot(p.astype(BF16), vw, preferred_element_type=F32)
    o_ref[...] = (o / l).astype(o_ref.dtype)


def _na_bias(rpb, grid_rows, q_rows, k_rows):
    n_blk = grid_rows // q_rows
    blocks = np.array([0, 1, n_blk - 1])
    a = np.arange(q_rows)
    r = blocks[:, None] * q_rows + a[None, :]
    kb = np.clip(blocks * q_rows - NA_ROWS // 2, 0, grid_rows - k_rows)
    kr = kb[:, None] + np.arange(k_rows)[None, :]
    rs = np.clip(r - NA_ROWS // 2, 0, grid_rows - NA_ROWS)
    row_ok = (kr[:, None, :] >= rs[:, :, None]) & (kr[:, None, :] < rs[:, :, None] + NA_ROWS)
    roff = kr[:, None, :] - r[:, :, None] + NA_ROWS - 1
    col = np.arange(GRID_W)
    cs = np.clip(col - NA_COLS // 2, 0, GRID_W - NA_COLS)
    col_ok = (col[None, :] >= cs[:, None]) & (col[None, :] < cs[:, None] + NA_COLS)
    coff = col[None, :] - col[:, None] + NA_COLS - 1
    n_c = 2 * NA_COLS - 1
    ok = row_ok[:, :, None, :, None] & col_ok[None, None, :, None, :]
    idx = roff[:, :, None, :, None] * n_c + coff[None, None, :, None, :]
    idx = np.where(ok, idx, 0).reshape(3, q_rows * GRID_W, k_rows * GRID_W)
    ok = np.broadcast_to(ok, (3, q_rows, GRID_W, k_rows, GRID_W)).reshape(idx.shape)
    flat = rpb.reshape(rpb.shape[0], -1)
    vals = jnp.take(flat, jnp.asarray(idx, jnp.int32), axis=1)
    return jnp.where(jnp.asarray(ok)[None], vals, NEG_INF)


def _na_attention(qkv, bias, cfg, bsz, seq):
    grid_rows = seq // GRID_W
    q_rows, k_rows = 8, 16
    tq, tkw = q_rows * GRID_W, k_rows * GRID_W
    n_blk = grid_rows // q_rows
    heads = cfg["heads_na"]
    k0, v0 = cfg["k0"], cfg["v0"]

    def btype(i):
        return jnp.where(i == 0, 0, jnp.where(i == n_blk - 1, 2, 1))

    kern = functools.partial(_na_kernel, grid_rows=grid_rows, q_rows=q_rows, k_rows=k_rows)
    return pl.pallas_call(
        kern,
        out_shape=jax.ShapeDtypeStruct((bsz * seq, heads * HEAD_DIM), BF16),
        grid=(bsz, heads, n_blk),
        in_specs=[
            pl.BlockSpec((tq, HEAD_DIM), lambda b, h, i: (b * n_blk + i, h)),
            pl.BlockSpec((seq, HEAD_DIM), lambda b, h, i: (b, k0 + h)),
            pl.BlockSpec((seq, HEAD_DIM), lambda b, h, i: (b, v0 + h)),
            pl.BlockSpec((1, 1, tq, tkw), lambda b, h, i: (h, btype(i), 0, 0)),
        ],
        out_specs=pl.BlockSpec((tq, HEAD_DIM), lambda b, h, i: (b * n_blk + i, h)),
        compiler_params=_params(
            ("parallel", "parallel", "arbitrary"),
            2 * (2 * seq * HEAD_DIM * 2 + tq * tkw * 4 + 2 * tq * HEAD_DIM * 2) + 4 * tq * tkw * 4),
        name="na_attn",
    )(qkv, qkv, qkv, bias)


def _t5_bucket_np(rel):
    nb = T5_BUCKETS // 2
    max_exact = nb // 2
    ret = np.where(rel > 0, nb, 0)
    n = np.abs(rel)
    nf = np.maximum(n, 1).astype(np.float64)
    large = max_exact + (np.log(nf / max_exact) / math.log(T5_MAX_DIST / max_exact)
                         * (nb - max_exact)).astype(np.int64)
    large = np.minimum(large, nb - 1)
    return ret + np.where(n < max_exact, n, large)


def _dil_bias(t5_table, tq, n_side):
    reach = tq * (n_side + 1)
    rel = np.arange(-reach + 1, reach)
    count = np.zeros(rel.shape, np.int64)
    for window, d in DIL_PATTERNS:
        half = window // (2 * d)
        count += ((rel % d) == 0) & (np.abs(rel) <= half * d)
    bucket = _t5_bucket_np(rel)
    logc = np.where(count > 0, np.log(np.maximum(count, 1)), 0.0).astype(np.float32)
    f = t5_table.T[:, bucket] + jnp.asarray(logc)[None, :]
    f = jnp.where(jnp.asarray(count > 0)[None, :], f, NEG_INF)
    qi = np.arange(tq)[:, None]
    kj = np.arange(tq)[None, :]
    chunks = np.arange(-n_side, n_side + 1)[:, None, None]
    idx = chunks * tq + kj - qi + reach - 1
    return jnp.take(f, jnp.asarray(idx, jnp.int32), axis=1)


def _dil_kernel(q_ref, k_ref, v_ref, b_ref, o_ref, *, tq, n_side, n_blk):
    i = pl.program_id(2)
    q = q_ref[...]

    def chunk(c):
        blk = jnp.clip(i + c, 0, n_blk - 1)
        start = pl.multiple_of(blk * tq, tq)
        s = _qk_scores(q, k_ref[pl.ds(start, tq), :]) + b_ref[0, c + n_side]
        if c != 0:
            in_range = (i + c >= 0) & (i + c < n_blk)
            s = s + jnp.where(in_range, 0.0, NEG_INF).astype(F32)
        return s, v_ref[pl.ds(start, tq), :]

    s, vc = chunk(0)
    m = jnp.max(s, axis=-1, keepdims=True)
    p = jnp.exp(s - m)
    l = jnp.sum(p, axis=-1, keepdims=True)
    acc = jnp.dot(p.astype(BF16), vc, preferred_element_type=F32)
    for c in [c for c in range(-n_side, n_side + 1) if c != 0]:
        s, vc = chunk(c)
        m_new = jnp.maximum(m, jnp.max(s, axis=-1, keepdims=True))
        alpha = jnp.exp(m - m_new)
        p = jnp.exp(s - m_new)
        l = alpha * l + jnp.sum(p, axis=-1, keepdims=True)
        acc = alpha * acc + jnp.dot(p.astype(BF16), vc, preferred_element_type=F32)
        m = m_new
    o_ref[...] = (acc / l).astype(o_ref.dtype)


def _dil_attention(qkv, bias, cfg, bsz, seq):
    tq = bias.shape[-1]
    n_side = (bias.shape[1] - 1) // 2
    n_blk = seq // tq
    heads = cfg["heads_dil"]
    q0 = cfg["heads_na"]
    k0, v0 = cfg["k0"] + cfg["heads_na"], cfg["v0"] + cfg["heads_na"]
    kern = functools.partial(_dil_kernel, tq=tq, n_side=n_side, n_blk=n_blk)
    n_chunks = 2 * n_side + 1
    return pl.pallas_call(
        kern,
        out_shape=jax.ShapeDtypeStruct((bsz * seq, heads * HEAD_DIM), BF16),
        grid=(bsz, heads, n_blk),
        in_specs=[
            pl.BlockSpec((tq, HEAD_DIM), lambda b, h, i: (b * n_blk + i, q0 + h)),
            pl.BlockSpec((seq, HEAD_DIM), lambda b, h, i: (b, k0 + h)),
            pl.BlockSpec((seq, HEAD_DIM), lambda b, h, i: (b, v0 + h)),
            pl.BlockSpec((1, n_chunks, tq, tq), lambda b, h, i: (h, 0, 0, 0)),
        ],
        out_specs=pl.BlockSpec((tq, HEAD_DIM), lambda b, h, i: (b * n_blk + i, h)),
        compiler_params=_params(
            ("parallel", "parallel", "arbitrary"),
            2 * (2 * seq * HEAD_DIM * 2 + n_chunks * tq * tq * 4 + 2 * tq * HEAD_DIM * 2)
            + 6 * tq * tq * 4),
        name="dil_attn",
    )(qkv, qkv, qkv, bias)


def _gqa_kernel(q0_ref, q1_ref, q2_ref, k_ref, v_ref, o_ref, *, tq, tk, seq):
    q = jnp.concatenate([q0_ref[...], q1_ref[...], q2_ref[...]], axis=0)

    def body(j, carry):
        m, l, acc = carry
        start = pl.multiple_of(j * tk, tk)
        s = _qk_scores(q, k_ref[pl.ds(start, tk), :])
        m_new = jnp.maximum(m, jnp.max(s, axis=-1, keepdims=True))
        alpha = jnp.exp(m - m_new)
        p = jnp.exp(s - m_new)
        l = alpha * l + jnp.sum(p, axis=-1, keepdims=True)
        acc = alpha * acc + jnp.dot(p.astype(BF16), v_ref[pl.ds(start, tk), :],
                                    preferred_element_type=F32)
        return m_new, l, acc

    rows = 3 * tq
    init = (jnp.full((rows, 1), NEG_INF, F32), jnp.zeros((rows, 1), F32),
            jnp.zeros((rows, HEAD_DIM), F32))
    _, l, acc = lax.fori_loop(0, seq // tk, body, init)
    o = acc / l
    for g in range(3):
        o_ref[:, g * HEAD_DIM:(g + 1) * HEAD_DIM] = o[g * tq:(g + 1) * tq].astype(o_ref.dtype)


def _gqa_attention(qkv, cfg, bsz, seq):
    tq, tk = 256, 512
    n_blk = seq // tq
    group = cfg["gqa_group"]
    assert group == 3
    kv_heads = cfg["kv_heads_gqa"]
    heads = kv_heads * group
    q0 = cfg["q_gqa0"]
    k0, v0 = cfg["k_gqa0"], cfg["v_gqa0"]
    kern = functools.partial(_gqa_kernel, tq=tq, tk=tk, seq=seq)
    q_spec = lambda g: pl.BlockSpec(
        (tq, HEAD_DIM), lambda b, kh, i: (b * n_blk + i, q0 + kh * group + g))
    return pl.pallas_call(
        kern,
        out_shape=jax.ShapeDtypeStruct((bsz * seq, heads * HEAD_DIM), BF16),
        grid=(bsz, kv_heads, n_blk),
        in_specs=[q_spec(0), q_spec(1), q_spec(2),
                  pl.BlockSpec((seq, HEAD_DIM), lambda b, kh, i: (b, k0 + kh)),
                  pl.BlockSpec((seq, HEAD_DIM), lambda b, kh, i: (b, v0 + kh))],
        out_specs=pl.BlockSpec((tq, group * HEAD_DIM), lambda b, kh, i: (b * n_blk + i, kh)),
        compiler_params=_params(
            ("parallel", "parallel", "arbitrary"),
            2 * (2 * seq * HEAD_DIM * 2 + 12 * tq * HEAD_DIM * 2) + 6 * 3 * tq * tk * 4),
        name="gqa_attn",
    )(qkv, qkv, qkv, qkv, qkv)


def _rope_tables(seq):
    half = HEAD_DIM // 2
    inv = np.exp(-math.log(ROPE_THETA) * np.arange(0, half, 2, dtype=np.float64) / half)
    t = np.arange(seq)
    ang_r = (t // GRID_W)[:, None] * inv[None, :]
    ang_c = (t % GRID_W)[:, None] * inv[None, :]
    cos = np.concatenate([np.cos(ang_r)] * 2 + [np.cos(ang_c)] * 2, axis=-1)
    sin = np.concatenate([-np.sin(ang_r), np.sin(ang_r), -np.sin(ang_c), np.sin(ang_c)], axis=-1)
    return jnp.asarray(cos, F32), jnp.asarray(sin, F32)


def _head_config(d_model, na_rpb, t5_table):
    n_heads = d_model // HEAD_DIM
    heads_na = na_rpb.shape[1]
    heads_dil = t5_table.shape[1]
    heads_gqa = n_heads - heads_na - heads_dil
    kv_heads_gqa = heads_gqa // 3
    k0 = n_heads
    v0 = k0 + heads_na + heads_dil + kv_heads_gqa
    return dict(
        n_heads=n_heads, heads_na=heads_na, heads_dil=heads_dil, gqa_group=3,
        kv_heads_gqa=kv_heads_gqa, q_gqa0=heads_na + heads_dil,
        k0=k0, k_gqa0=k0 + heads_na + heads_dil, v0=v0, v_gqa0=v0 + heads_na + heads_dil)


def kernel(x, c, ada_w, ada_b, ada_layer_emb, norm_gains, w_in, w_out, q_gain, k_gain, na_rpb,
           t5_table, w_mlp_in, w_mlp_out):
    bsz, seq, d = x.shape
    depth = w_in.shape[0]
    cfg = _head_config(d, na_rpb, t5_table)
    grid_rows = seq // GRID_W

    mod = _ada_mod(c, ada_w, ada_b).reshape(bsz, N_MOD, d)
    cos_t, sin_t = _rope_tables(seq)
    dil_bias = _dil_bias(t5_table, tq=512, n_side=2)

    x2 = x.reshape(bsz * seq, d)
    for l in range(depth):
        m = mod + ada_layer_emb[l][None]
        h = _norm_mod(x2, norm_gains[l, 0], m, 0, 1, seq)
        qkv = _qkv_proj(h, w_in[l].astype(BF16), cos_t, sin_t, q_gain[l], k_gain[l], cfg, seq)
        na_bias = _na_bias(na_rpb[l], grid_rows, 8, 16)
        o_na = _na_attention(qkv, na_bias, cfg, bsz, seq)
        o_dil = _dil_attention(qkv, dil_bias, cfg, bsz, seq)
        o_gqa = _gqa_attention(qkv, cfg, bsz, seq)
        y = _out_proj(o_na, o_dil, o_gqa, w_out[l].astype(BF16))
        x2 = _resid_norm(x2, y, norm_gains[l, 1], m, 2, seq)
        h = _norm_mod(x2, norm_gains[l, 2], m, 3, 4, seq)
        u = _mlp_in(h, w_mlp_in[l].astype(BF16))
        y = _mlp_out(u, w_mlp_out[l].astype(BF16))
        x2 = _resid_norm(x2, y, norm_gains[l, 3], m, 5, seq)
    return x2.reshape(bsz, seq, d)
```

```python
import functools
import math

import numpy as np
import jax
import jax.numpy as jnp
from jax import lax
from jax.experimental import pallas as pl
from jax.experimental.pallas import tpu as pltpu

HEAD_DIM = 128
GRID_W = 64
NA_ROWS = 8
NA_COLS = 16
DIL_PATTERNS = ((128, 1), (512, 4), (2048, 16))
T5_BUCKETS = 32
T5_MAX_DIST = 1024
ROPE_THETA = 10000.0
N_MOD = 6
EPS = 1e-6
NEG_INF = -1e30
ATTN_SCALE = HEAD_DIM ** -0.5
LOG2E = math.log2(math.e)

V7X_VMEM_BYTES = 64 * 1024 * 1024
V7X_LANES = 128

BF16 = jnp.bfloat16
F32 = jnp.float32


def _params(semantics, vmem_bytes):
    limit = min(int(vmem_bytes * 1.25) + (4 << 20), V7X_VMEM_BYTES - (6 << 20))
    return pltpu.CompilerParams(dimension_semantics=semantics, vmem_limit_bytes=limit)


def _ada_kernel(c_ref, w_ref, b_ref, o_ref):
    c = c_ref[...]
    a = c * jax.nn.sigmoid(c)
    a_hi = a.astype(BF16)
    a_lo = (a - a_hi.astype(F32)).astype(BF16)
    w = w_ref[...]
    w_hi = w.astype(BF16)
    w_lo = (w - w_hi.astype(F32)).astype(BF16)
    acc = jnp.dot(a_hi, w_hi, preferred_element_type=F32)
    acc += jnp.dot(a_hi, w_lo, preferred_element_type=F32)
    acc += jnp.dot(a_lo, w_hi, preferred_element_type=F32)
    o_ref[...] = acc + b_ref[...]


def _ada_mod(c, ada_w, ada_b):
    bsz, d = c.shape
    n = ada_w.shape[1]
    rows = 8
    tn = 512
    c_pad = jnp.zeros((rows, d), F32).at[:bsz].set(c)
    out = pl.pallas_call(
        _ada_kernel,
        out_shape=jax.ShapeDtypeStruct((rows, n), F32),
        grid=(n // tn,),
        in_specs=[
            pl.BlockSpec((rows, d), lambda j: (0, 0)),
            pl.BlockSpec((d, tn), lambda j: (0, j)),
            pl.BlockSpec((1, tn), lambda j: (0, j)),
        ],
        out_specs=pl.BlockSpec((rows, tn), lambda j: (0, j)),
        compiler_params=_params(("arbitrary",), 2 * d * tn * 4 + 3 * d * tn * 2),
        name="ada_mod",
    )(c_pad, ada_w, ada_b.reshape(1, n))
    return out[:bsz]


def _norm_mod_kernel(x_ref, g_ref, m_ref, o_ref, *, shift_row, scale_row):
    x = x_ref[...]
    y = x * lax.rsqrt(jnp.mean(x * x, axis=-1, keepdims=True) + EPS) * g_ref[...]
    shift = m_ref[0, shift_row:shift_row + 1, :]
    scale = m_ref[0, scale_row:scale_row + 1, :]
    o_ref[...] = (y * (1.0 + scale) + shift).astype(o_ref.dtype)


def _norm_mod(x2, gain, m, shift_row, scale_row, seq):
    rows, d = x2.shape
    tr = 256
    per_b = seq // tr
    return pl.pallas_call(
        functools.partial(_norm_mod_kernel, shift_row=shift_row, scale_row=scale_row),
        out_shape=jax.ShapeDtypeStruct((rows, d), BF16),
        grid=(rows // tr,),
        in_specs=[
            pl.BlockSpec((tr, d), lambda i: (i, 0)),
            pl.BlockSpec((1, d), lambda i: (0, 0)),
            pl.BlockSpec((1, N_MOD, d), lambda i: (i // per_b, 0, 0)),
        ],
        out_specs=pl.BlockSpec((tr, d), lambda i: (i, 0)),
        compiler_params=_params(("parallel",), 2 * tr * d * 6 + 4 * tr * d * 4),
        name="norm_mod",
    )(x2, gain.reshape(1, d), m)


def _resid_norm_kernel(x_ref, y_ref, g_ref, m_ref, o_ref, *, gate_row):
    y = y_ref[...]
    yn = y * lax.rsqrt(jnp.mean(y * y, axis=-1, keepdims=True) + EPS) * g_ref[...]
    gate = m_ref[0, gate_row:gate_row + 1, :]
    o_ref[...] = x_ref[...] + gate * yn


def _resid_norm(x2, y2, gain, m, gate_row, seq):
    rows, d = x2.shape
    tr = 256
    per_b = seq // tr
    return pl.pallas_call(
        functools.partial(_resid_norm_kernel, gate_row=gate_row),
        out_shape=jax.ShapeDtypeStruct((rows, d), F32),
        grid=(rows // tr,),
        in_specs=[
            pl.BlockSpec((tr, d), lambda i: (i, 0)),
            pl.BlockSpec((tr, d), lambda i: (i, 0)),
            pl.BlockSpec((1, d), lambda i: (0, 0)),
            pl.BlockSpec((1, N_MOD, d), lambda i: (i // per_b, 0, 0)),
        ],
        out_specs=pl.BlockSpec((tr, d), lambda i: (i, 0)),
        compiler_params=_params(("parallel",), 2 * tr * d * 12 + 4 * tr * d * 4),
        name="resid_norm",
    )(x2, y2, gain.reshape(1, d), m)


def _resid_norm_mod_kernel(x_ref, y_ref, g1_ref, g2_ref, ma_ref, mb_ref, xo_ref, ho_ref, *,
                           gate_row, shift_row, scale_row):
    y = y_ref[...]
    yn = y * lax.rsqrt(jnp.mean(y * y, axis=-1, keepdims=True) + EPS) * g1_ref[...]
    x = x_ref[...] + ma_ref[0, gate_row:gate_row + 1, :] * yn
    xo_ref[...] = x
    h = x * lax.rsqrt(jnp.mean(x * x, axis=-1, keepdims=True) + EPS) * g2_ref[...]
    shift = mb_ref[0, shift_row:shift_row + 1, :]
    scale = mb_ref[0, scale_row:scale_row + 1, :]
    ho_ref[...] = (h * (1.0 + scale) + shift).astype(ho_ref.dtype)


def _resid_norm_mod(x2, y2, gain1, gain2, m_gate, m_mod, gate_row, shift_row, scale_row, seq):
    rows, d = x2.shape
    tr = 256
    per_b = seq // tr
    row_spec = pl.BlockSpec((tr, d), lambda i: (i, 0))
    vec_spec = pl.BlockSpec((1, d), lambda i: (0, 0))
    mod_spec = pl.BlockSpec((1, N_MOD, d), lambda i: (i // per_b, 0, 0))
    return pl.pallas_call(
        functools.partial(_resid_norm_mod_kernel, gate_row=gate_row, shift_row=shift_row,
                          scale_row=scale_row),
        out_shape=(jax.ShapeDtypeStruct((rows, d), F32), jax.ShapeDtypeStruct((rows, d), BF16)),
        grid=(rows // tr,),
        in_specs=[row_spec, row_spec, vec_spec, vec_spec, mod_spec, mod_spec],
        out_specs=(row_spec, row_spec),
        compiler_params=_params(("parallel",), 2 * tr * d * 14 + 5 * tr * d * 4),
        name="resid_norm_mod",
    )(x2, y2, gain1.reshape(1, d), gain2.reshape(1, d), m_gate, m_mod)


def _swap_half_pairs(y):
    lane = lax.broadcasted_iota(jnp.int32, y.shape, 1)
    fwd = pltpu.roll(y, HEAD_DIM - 32, axis=1)
    bwd = pltpu.roll(y, 32, axis=1)
    return jnp.where((lane % 64) < 32, fwd, bwd)


def _qkv_kernel(a_ref, w_ref, cos_ref, sin_ref, qg_ref, kg_ref, o_ref, *,
                n_q_blocks, gq_lo, gq_hi, gk_lo, gk_hi, heads_per_block):
    j = pl.program_id(1)
    acc = jnp.dot(a_ref[...], w_ref[...], preferred_element_type=F32)
    is_q = j < n_q_blocks
    is_gq = (j >= gq_lo) & (j < gq_hi)
    is_gk = (j >= gk_lo) & (j < gk_hi)
    rope = is_gq | is_gk
    out_scale = jnp.where(is_q, ATTN_SCALE * LOG2E, 1.0).astype(F32)

    @pl.when(rope)
    def _():
        gain = jnp.where(is_gq, qg_ref[...], kg_ref[...])
        cos = cos_ref[...]
        sin = sin_ref[...]
        for h in range(heads_per_block):
            xh = acc[:, h * HEAD_DIM:(h + 1) * HEAD_DIM]
            y = xh * lax.rsqrt(jnp.mean(xh * xh, axis=-1, keepdims=True) + EPS) * gain
            y = y * cos + _swap_half_pairs(y) * sin
            o_ref[:, h * HEAD_DIM:(h + 1) * HEAD_DIM] = (y * out_scale).astype(o_ref.dtype)

    @pl.when(jnp.logical_not(rope))
    def _():
        o_ref[...] = (acc * out_scale).astype(o_ref.dtype)


def _qkv_proj(h2, w, cos_t, sin_t, q_gain, k_gain, cfg, seq):
    m, k = h2.shape
    n = w.shape[1]
    tm, tn = 1024, 512
    hpb = tn // HEAD_DIM
    blk = lambda head: head // hpb
    kern = functools.partial(
        _qkv_kernel,
        n_q_blocks=blk(cfg["n_heads"]),
        gq_lo=blk(cfg["q_gqa0"]), gq_hi=blk(cfg["n_heads"]),
        gk_lo=blk(cfg["k_gqa0"]), gk_hi=blk(cfg["v0"]),
        heads_per_block=hpb)
    per_b = seq // tm
    return pl.pallas_call(
        kern,
        out_shape=jax.ShapeDtypeStruct((m, n), BF16),
        grid=(m // tm, n // tn),
        in_specs=[
            pl.BlockSpec((tm, k), lambda i, j: (i, 0)),
            pl.BlockSpec((k, tn), lambda i, j: (0, j)),
            pl.BlockSpec((tm, HEAD_DIM), lambda i, j: (i % per_b, 0)),
            pl.BlockSpec((tm, HEAD_DIM), lambda i, j: (i % per_b, 0)),
            pl.BlockSpec((1, HEAD_DIM), lambda i, j: (0, 0)),
            pl.BlockSpec((1, HEAD_DIM), lambda i, j: (0, 0)),
        ],
        out_specs=pl.BlockSpec((tm, tn), lambda i, j: (i, j)),
        compiler_params=_params(
            ("parallel", "arbitrary"),
            2 * (tm * k * 2 + k * tn * 2 + tm * tn * 2 + 2 * tm * HEAD_DIM * 4) + 3 * tm * tn * 4),
        name="qkv_proj",
    )(h2, w, cos_t, sin_t, q_gain.reshape(1, HEAD_DIM), k_gain.reshape(1, HEAD_DIM))


def _out_proj_kernel(a0_ref, a1_ref, a2_ref, w_ref, o_ref, *, splits):
    s0, s1, s2 = splits
    acc = jnp.dot(a0_ref[...], w_ref[0:s0, :], preferred_element_type=F32)
    acc += jnp.dot(a1_ref[...], w_ref[s0:s0 + s1, :], preferred_element_type=F32)
    acc += jnp.dot(a2_ref[...], w_ref[s0 + s1:s0 + s1 + s2, :], preferred_element_type=F32)
    o_ref[...] = acc


def _out_proj(o_na, o_dil, o_gqa, w):
    m = o_na.shape[0]
    k, n = w.shape
    splits = (o_na.shape[1], o_dil.shape[1], o_gqa.shape[1])
    tm, tn = 1024, 1024
    return pl.pallas_call(
        functools.partial(_out_proj_kernel, splits=splits),
        out_shape=jax.ShapeDtypeStruct((m, n), F32),
        grid=(m // tm, n // tn),
        in_specs=[
            pl.BlockSpec((tm, splits[0]), lambda i, j: (i, 0)),
            pl.BlockSpec((tm, splits[1]), lambda i, j: (i, 0)),
            pl.BlockSpec((tm, splits[2]), lambda i, j: (i, 0)),
            pl.BlockSpec((k, tn), lambda i, j: (0, j)),
        ],
        out_specs=pl.BlockSpec((tm, tn), lambda i, j: (i, j)),
        compiler_params=_params(
            ("parallel", "arbitrary"),
            2 * (tm * k * 2 + k * tn * 2 + tm * tn * 4) + 2 * tm * tn * 4),
        name="out_proj",
    )(o_na, o_dil, o_gqa, w)


def _mlp_in_kernel(a_ref, w_ref, o_ref):
    acc = jnp.dot(a_ref[...], w_ref[...], preferred_element_type=F32)
    r = jnp.maximum(acc, 0.0)
    o_ref[...] = (r * r).astype(o_ref.dtype)


def _mlp_in(h2, w):
    m, k = h2.shape
    n = w.shape[1]
    tm, tn = 1024, 1024
    return pl.pallas_call(
        _mlp_in_kernel,
        out_shape=jax.ShapeDtypeStruct((m, n), BF16),
        grid=(m // tm, n // tn),
        in_specs=[
            pl.BlockSpec((tm, k), lambda i, j: (i, 0)),
            pl.BlockSpec((k, tn), lambda i, j: (0, j)),
        ],
        out_specs=pl.BlockSpec((tm, tn), lambda i, j: (i, j)),
        compiler_params=_params(
            ("parallel", "arbitrary"),
            2 * (tm * k * 2 + k * tn * 2 + tm * tn * 2) + 2 * tm * tn * 4),
        name="mlp_in",
    )(h2, w)


def _mlp_out_kernel(a_ref, w_ref, o_ref):
    kk = pl.program_id(2)
    acc = jnp.dot(a_ref[...], w_ref[...], preferred_element_type=F32)

    @pl.when(kk == 0)
    def _():
        o_ref[...] = acc

    @pl.when(kk != 0)
    def _():
        o_ref[...] += acc


def _mlp_out(u2, w):
    m, k = u2.shape
    n = w.shape[1]
    tm, tn, tk = 1024, 1024, 4096
    return pl.pallas_call(
        _mlp_out_kernel,
        out_shape=jax.ShapeDtypeStruct((m, n), F32),
        grid=(m // tm, n // tn, k // tk),
        in_specs=[
            pl.BlockSpec((tm, tk), lambda i, j, kk: (i, kk)),
            pl.BlockSpec((tk, tn), lambda i, j, kk: (kk, j)),
        ],
        out_specs=pl.BlockSpec((tm, tn), lambda i, j, kk: (i, j)),
        compiler_params=_params(
            ("parallel", "arbitrary", "arbitrary"),
            2 * (tm * tk * 2 + tk * tn * 2 + tm * tn * 4) + 2 * tm * tn * 4),
        name="mlp_out",
    )(u2, w)


def _qk_scores(q, k):
    return lax.dot_general(q, k, (((1,), (1,)), ((), ())), preferred_element_type=F32)


def _toeplitz_rows(row, n_rows):
    x = jnp.broadcast_to(row, (n_rows, row.shape[-1]))
    return pltpu.roll(x, 0, axis=1, stride=1, stride_axis=0)


def _na_kernel(q_ref, k_ref, v_ref, w_ref, cm_ref, o_ref, bias_ref, *, grid_rows, q_rows, k_rows,
               n_blk):
    i = pl.program_id(2)

    @pl.when((i == 0) | (i == 1) | (i == n_blk - 1))
    def _():
        for a in range(q_rows):
            band = _toeplitz_rows(w_ref[0, 0, a], GRID_W)
            bias_ref[a * GRID_W:(a + 1) * GRID_W, :] = band + cm_ref[...]

    kb = jnp.clip(i * q_rows - NA_ROWS // 2, 0, grid_rows - k_rows) * GRID_W
    kb = pl.multiple_of(kb, 256)
    q = q_ref[...]
    kw = k_ref[pl.ds(kb, k_rows * GRID_W), :]
    vw = v_ref[pl.ds(kb, k_rows * GRID_W), :]
    s = _qk_scores(q, kw) + bias_ref[...]
    m = jnp.max(s, axis=-1, keepdims=True)
    p = jnp.exp2(s - m)
    l = jnp.sum(p, axis=-1, keepdims=True)
    o = jnp.dot(p.astype(BF16), vw, preferred_element_type=F32)
    o_ref[...] = (o / l).astype(o_ref.dtype)


def _na_tables(rpb, grid_rows, q_rows, k_rows):
    n_blk = grid_rows // q_rows
    width = k_rows * GRID_W
    blocks = np.array([0, 1, n_blk - 1])
    a = np.arange(q_rows)
    r = blocks[:, None] * q_rows + a[None, :]
    kb = np.clip(blocks * q_rows - NA_ROWS // 2, 0, grid_rows - k_rows)
    rs = np.clip(r - NA_ROWS // 2, 0, grid_rows - NA_ROWS)
    j = np.arange(width)
    krl = ((j + GRID_W // 2) // GRID_W) % k_rows
    e = (j + GRID_W // 2) % GRID_W - GRID_W // 2
    kr = kb[:, None, None] + krl[None, None, :]
    row_ok = (kr >= rs[:, :, None]) & (kr < rs[:, :, None] + NA_ROWS)
    roff = kr - r[:, :, None] + NA_ROWS - 1
    ok = row_ok & (np.abs(e) <= NA_COLS - 1)[None, None, :]
    n_c = 2 * NA_COLS - 1
    idx = np.where(ok, roff * n_c + (e + NA_COLS - 1)[None, None, :], 0)
    flat = rpb.reshape(rpb.shape[0], -1)
    vals = jnp.take(flat, jnp.asarray(idx.reshape(-1), jnp.int32), axis=1)
    vals = vals.reshape(rpb.shape[0], 3, q_rows, 1, width)
    tab = jnp.where(jnp.asarray(ok)[None, :, :, None, :], vals * LOG2E, NEG_INF)
    col = np.arange(GRID_W)
    cs = np.clip(col - NA_COLS // 2, 0, GRID_W - NA_COLS)
    col_ok = (col[None, :] >= cs[:, None]) & (col[None, :] < cs[:, None] + NA_COLS)
    cmask = np.where(np.tile(col_ok, (1, k_rows)), 0.0, NEG_INF).astype(np.float32)
    return tab, jnp.asarray(cmask)


def _na_attention(qkv, tab, cmask, cfg, bsz, seq):
    grid_rows = seq // GRID_W
    q_rows, k_rows = tab.shape[2], cmask.shape[1] // GRID_W
    tq, tkw = q_rows * GRID_W, k_rows * GRID_W
    n_blk = grid_rows // q_rows
    assert n_blk >= 3
    heads = cfg["heads_na"]
    k0, v0 = cfg["k0"], cfg["v0"]

    def btype(i):
        return jnp.where(i == 0, 0, jnp.where(i == n_blk - 1, 2, 1))

    kern = functools.partial(_na_kernel, grid_rows=grid_rows, q_rows=q_rows, k_rows=k_rows,
                             n_blk=n_blk)
    return pl.pallas_call(
        kern,
        out_shape=jax.ShapeDtypeStruct((bsz * seq, heads * HEAD_DIM), BF16),
        grid=(bsz, heads, n_blk),
        in_specs=[
            pl.BlockSpec((tq, HEAD_DIM), lambda b, h, i: (b * n_blk + i, h)),
            pl.BlockSpec((seq, HEAD_DIM), lambda b, h, i: (b, k0 + h)),
            pl.BlockSpec((seq, HEAD_DIM), lambda b, h, i: (b, v0 + h)),
            pl.BlockSpec((1, 1, q_rows, 1, tkw), lambda b, h, i: (h, btype(i), 0, 0, 0)),
            pl.BlockSpec((GRID_W, tkw), lambda b, h, i: (0, 0)),
        ],
        out_specs=pl.BlockSpec((tq, HEAD_DIM), lambda b, h, i: (b * n_blk + i, h)),
        scratch_shapes=[pltpu.VMEM((tq, tkw), F32)],
        compiler_params=_params(
            ("parallel", "parallel", "arbitrary"),
            2 * (2 * seq * HEAD_DIM * 2 + 2 * tq * HEAD_DIM * 2) + 6 * tq * tkw * 4),
        name="na_attn",
    )(qkv, qkv, qkv, tab, cmask)


def _t5_bucket_np(rel):
    nb = T5_BUCKETS // 2
    max_exact = nb // 2
    ret = np.where(rel > 0, nb, 0)
    n = np.abs(rel)
    nf = np.maximum(n, 1).astype(np.float64)
    large = max_exact + (np.log(nf / max_exact) / math.log(T5_MAX_DIST / max_exact)
                         * (nb - max_exact)).astype(np.int64)
    large = np.minimum(large, nb - 1)
    return ret + np.where(n < max_exact, n, large)


def _dil_tables(t5_table, tq, n_side):
    n_chunks = 2 * n_side + 1
    rel = ((np.arange(n_chunks) - n_side)[:, None] * tq + np.arange(2 * tq)[None, :] - tq).reshape(-1)
    count = np.zeros(rel.shape, np.int64)
    for window, d in DIL_PATTERNS:
        half = window // (2 * d)
        count += ((rel % d) == 0) & (np.abs(rel) <= half * d)
    bucket = _t5_bucket_np(rel)
    logc = np.log(np.maximum(count, 1)).astype(np.float32)
    f = (t5_table.T[:, bucket] + jnp.asarray(logc)[None, :]) * LOG2E
    f = jnp.where(jnp.asarray(count > 0)[None, :], f, NEG_INF)
    return f.reshape(t5_table.shape[1], n_chunks, 1, 2 * tq)


def _dil_kernel(q_ref, k_ref, v_ref, w_ref, o_ref, bias_ref, *, tq, n_side, n_blk):
    i = pl.program_id(2)

    @pl.when(i == 0)
    def _():
        for c in range(2 * n_side + 1):
            bias_ref[c] = _toeplitz_rows(w_ref[0, c], tq)[:, tq:2 * tq]

    q = q_ref[...]

    def chunk(c):
        blk = jnp.clip(i + c, 0, n_blk - 1)
        start = pl.multiple_of(blk * tq, tq)
        s = _qk_scores(q, k_ref[pl.ds(start, tq), :]) + bias_ref[c + n_side]
        if c != 0:
            in_range = (i + c >= 0) & (i + c < n_blk)
            s = s + jnp.where(in_range, 0.0, NEG_INF).astype(F32)
        return s, v_ref[pl.ds(start, tq), :]

    s, vc = chunk(0)
    m = jnp.max(s, axis=-1, keepdims=True)
    p = jnp.exp2(s - m)
    l = jnp.sum(p, axis=-1, keepdims=True)
    acc = jnp.dot(p.astype(BF16), vc, preferred_element_type=F32)
    for c in [c for c in range(-n_side, n_side + 1) if c != 0]:
        s, vc = chunk(c)
        m_new = jnp.maximum(m, jnp.max(s, axis=-1, keepdims=True))
        alpha = jnp.exp2(m - m_new)
        p = jnp.exp2(s - m_new)
        l = alpha * l + jnp.sum(p, axis=-1, keepdims=True)
        acc = alpha * acc + jnp.dot(p.astype(BF16), vc, preferred_element_type=F32)
        m = m_new
    o_ref[...] = (acc / l).astype(o_ref.dtype)


def _dil_attention(qkv, tab, cfg, bsz, seq):
    tq = tab.shape[-1] // 2
    n_chunks = tab.shape[1]
    n_side = (n_chunks - 1) // 2
    n_blk = seq // tq
    heads = cfg["heads_dil"]
    q0 = cfg["heads_na"]
    k0, v0 = cfg["k0"] + cfg["heads_na"], cfg["v0"] + cfg["heads_na"]
    kern = functools.partial(_dil_kernel, tq=tq, n_side=n_side, n_blk=n_blk)
    return pl.pallas_call(
        kern,
        out_shape=jax.ShapeDtypeStruct((bsz * seq, heads * HEAD_DIM), BF16),
        grid=(bsz, heads, n_blk),
        in_specs=[
            pl.BlockSpec((tq, HEAD_DIM), lambda b, h, i: (b * n_blk + i, q0 + h)),
            pl.BlockSpec((seq, HEAD_DIM), lambda b, h, i: (b, k0 + h)),
            pl.BlockSpec((seq, HEAD_DIM), lambda b, h, i: (b, v0 + h)),
            pl.BlockSpec((1, n_chunks, 1, 2 * tq), lambda b, h, i: (h, 0, 0, 0)),
        ],
        out_specs=pl.BlockSpec((tq, HEAD_DIM), lambda b, h, i: (b * n_blk + i, h)),
        scratch_shapes=[pltpu.VMEM((n_chunks, tq, tq), F32)],
        compiler_params=_params(
            ("parallel", "parallel", "arbitrary"),
            2 * (2 * seq * HEAD_DIM * 2 + 2 * tq * HEAD_DIM * 2) + n_chunks * tq * tq * 4
            + 2 * tq * 2 * tq * 4 + 6 * tq * tq * 4),
        name="dil_attn",
    )(qkv, qkv, qkv, tab)


def _gqa_kernel(q0_ref, q1_ref, q2_ref, k_ref, v_ref, o_ref, acc_ref, sa_ref, sb_ref, *,
                tq, tk, seq):
    q = jnp.concatenate([q0_ref[...], q1_ref[...], q2_ref[...]], axis=0)
    mq = 3 * tq
    n = seq // tk
    acc_ref[...] = jnp.zeros_like(acc_ref)

    def scores(j):
        start = pl.multiple_of(j * tk, tk)
        return lax.dot_general(k_ref[pl.ds(start, tk), :], q, (((1,), (1,)), ((), ())),
                               preferred_element_type=F32)

    def softmax_pv(j, s_ref, m, l):
        start = pl.multiple_of(j * tk, tk)
        st = s_ref[...]
        m_new = jnp.maximum(m, jnp.max(st, axis=0, keepdims=True))
        alpha = jnp.exp2(m - m_new)
        pt = jnp.exp2(st - m_new)
        l = alpha * l + jnp.sum(pt, axis=0, keepdims=True)
        pv = lax.dot_general(v_ref[pl.ds(start, tk), :], pt.astype(BF16),
                             (((0,), (0,)), ((), ())), preferred_element_type=F32)
        acc_ref[...] = alpha * acc_ref[...] + pv
        return m_new, l

    sa_ref[...] = scores(0)

    def body(jj, carry):
        m, l = carry
        j = 2 * jj
        sb_ref[...] = scores(j + 1)
        m, l = softmax_pv(j, sa_ref, m, l)
        sa_ref[...] = scores(jnp.minimum(j + 2, n - 1))
        return softmax_pv(j + 1, sb_ref, m, l)

    init = (jnp.full((1, mq), NEG_INF, F32), jnp.zeros((1, mq), F32))
    _, l = lax.fori_loop(0, n // 2, body, init)
    o = (acc_ref[...] / l).T
    for g in range(3):
        o_ref[:, g * HEAD_DIM:(g + 1) * HEAD_DIM] = o[g * tq:(g + 1) * tq].astype(o_ref.dtype)


def _gqa_attention(qkv, cfg, bsz, seq):
    tq, tk = 256, 512
    assert seq % (2 * tk) == 0
    n_blk = seq // tq
    group = cfg["gqa_group"]
    assert group == 3
    kv_heads = cfg["kv_heads_gqa"]
    heads = kv_heads * group
    q0 = cfg["q_gqa0"]
    k0, v0 = cfg["k_gqa0"], cfg["v_gqa0"]
    kern = functools.partial(_gqa_kernel, tq=tq, tk=tk, seq=seq)
    q_spec = lambda g: pl.BlockSpec(
        (tq, HEAD_DIM), lambda b, kh, i: (b * n_blk + i, q0 + kh * group + g))
    return pl.pallas_call(
        kern,
        out_shape=jax.ShapeDtypeStruct((bsz * seq, heads * HEAD_DIM), BF16),
        grid=(bsz, kv_heads, n_blk),
        in_specs=[q_spec(0), q_spec(1), q_spec(2),
                  pl.BlockSpec((seq, HEAD_DIM), lambda b, kh, i: (b, k0 + kh)),
                  pl.BlockSpec((seq, HEAD_DIM), lambda b, kh, i: (b, v0 + kh))],
        out_specs=pl.BlockSpec((tq, group * HEAD_DIM), lambda b, kh, i: (b * n_blk + i, kh)),
        scratch_shapes=[pltpu.VMEM((HEAD_DIM, group * tq), F32),
                        pltpu.VMEM((tk, group * tq), F32),
                        pltpu.VMEM((tk, group * tq), F32)],
        compiler_params=_params(
            ("parallel", "parallel", "arbitrary"),
            2 * (2 * seq * HEAD_DIM * 2 + 12 * tq * HEAD_DIM * 2) + 8 * group * tq * tk * 4),
        name="gqa_attn",
    )(qkv, qkv, qkv, qkv, qkv)


def _rope_tables(seq):
    half = HEAD_DIM // 2
    inv = np.exp(-math.log(ROPE_THETA) * np.arange(0, half, 2, dtype=np.float64) / half)
    t = np.arange(seq)
    ang_r = (t // GRID_W)[:, None] * inv[None, :]
    ang_c = (t % GRID_W)[:, None] * inv[None, :]
    cos = np.concatenate([np.cos(ang_r)] * 2 + [np.cos(ang_c)] * 2, axis=-1)
    sin = np.concatenate([-np.sin(ang_r), np.sin(ang_r), -np.sin(ang_c), np.sin(ang_c)], axis=-1)
    return jnp.asarray(cos, F32), jnp.asarray(sin, F32)


def _head_config(d_model, na_rpb, t5_table):
    n_heads = d_model // HEAD_DIM
    heads_na = na_rpb.shape[1]
    heads_dil = t5_table.shape[1]
    heads_gqa = n_heads - heads_na - heads_dil
    kv_heads_gqa = heads_gqa // 3
    k0 = n_heads
    v0 = k0 + heads_na + heads_dil + kv_heads_gqa
    return dict(
        n_heads=n_heads, heads_na=heads_na, heads_dil=heads_dil, gqa_group=3,
        kv_heads_gqa=kv_heads_gqa, q_gqa0=heads_na + heads_dil,
        k0=k0, k_gqa0=k0 + heads_na + heads_dil, v0=v0, v_gqa0=v0 + heads_na + heads_dil)


def kernel(x, c, ada_w, ada_b, ada_layer_emb, norm_gains, w_in, w_out, q_gain, k_gain, na_rpb,
           t5_table, w_mlp_in, w_mlp_out):
    bsz, seq, d = x.shape
    depth = w_in.shape[0]
    cfg = _head_config(d, na_rpb, t5_table)
    grid_rows = seq // GRID_W

    mod = _ada_mod(c, ada_w, ada_b).reshape(bsz, N_MOD, d)
    mods = [mod + ada_layer_emb[l][None] for l in range(depth)]
    cos_t, sin_t = _rope_tables(seq)
    dil_tab = _dil_tables(t5_table, tq=512, n_side=2)

    x2 = x.reshape(bsz * seq, d)
    h = _norm_mod(x2, norm_gains[0, 0], mods[0], 0, 1, seq)
    for l in range(depth):
        m = mods[l]
        qkv = _qkv_proj(h, w_in[l].astype(BF16), cos_t, sin_t, q_gain[l], k_gain[l], cfg, seq)
        na_tab, na_cmask = _na_tables(na_rpb[l], grid_rows, 8, 16)
        o_na = _na_attention(qkv, na_tab, na_cmask, cfg, bsz, seq)
        o_dil = _dil_attention(qkv, dil_tab, cfg, bsz, seq)
        o_gqa = _gqa_attention(qkv, cfg, bsz, seq)
        y = _out_proj(o_na, o_dil, o_gqa, w_out[l].astype(BF16))
        x2, h = _resid_norm_mod(x2, y, norm_gains[l, 1], norm_gains[l, 2], m, m, 2, 3, 4, seq)
        u = _mlp_in(h, w_mlp_in[l].astype(BF16))
        y = _mlp_out(u, w_mlp_out[l].astype(BF16))
        if l + 1 < depth:
            x2, h = _resid_norm_mod(x2, y, norm_gains[l, 3], norm_gains[l + 1, 0], m, mods[l + 1],
                                    5, 0, 1, seq)
        else:
            x2 = _resid_norm(x2, y, norm_gains[l, 3], m, 5, seq)
    return x2.reshape(bsz, seq, d)
```

```python
import functools
import math

import numpy as np
import jax
import jax.numpy as jnp
from jax import lax
from jax.experimental import pallas as pl
from jax.experimental.pallas import tpu as pltpu

HEAD_DIM = 128
GRID_W = 64
NA_ROWS = 8
NA_COLS = 16
DIL_PATTERNS = ((128, 1), (512, 4), (2048, 16))
T5_BUCKETS = 32
T5_MAX_DIST = 1024
ROPE_THETA = 10000.0
N_MOD = 6
EPS = 1e-6
NEG_INF = -1e30
ATTN_SCALE = HEAD_DIM ** -0.5
LOG2E = math.log2(math.e)

V7X_VMEM_BYTES = 64 * 1024 * 1024
V7X_LANES = 128

BF16 = jnp.bfloat16
F32 = jnp.float32


def _params(semantics, vmem_bytes):
    limit = min(int(vmem_bytes * 1.25) + (4 << 20), V7X_VMEM_BYTES - (6 << 20))
    return pltpu.CompilerParams(dimension_semantics=semantics, vmem_limit_bytes=limit)


def _ada_kernel(c_ref, w_ref, b_ref, o_ref):
    c = c_ref[...]
    a = c * jax.nn.sigmoid(c)
    a_hi = a.astype(BF16)
    a_lo = (a - a_hi.astype(F32)).astype(BF16)
    w = w_ref[...]
    w_hi = w.astype(BF16)
    w_lo = (w - w_hi.astype(F32)).astype(BF16)
    acc = jnp.dot(a_hi, w_hi, preferred_element_type=F32)
    acc += jnp.dot(a_hi, w_lo, preferred_element_type=F32)
    acc += jnp.dot(a_lo, w_hi, preferred_element_type=F32)
    o_ref[...] = acc + b_ref[...]


def _ada_mod(c, ada_w, ada_b):
    bsz, d = c.shape
    n = ada_w.shape[1]
    rows = 8
    tn = 512
    c_pad = jnp.zeros((rows, d), F32).at[:bsz].set(c)
    out = pl.pallas_call(
        _ada_kernel,
        out_shape=jax.ShapeDtypeStruct((rows, n), F32),
        grid=(n // tn,),
        in_specs=[
            pl.BlockSpec((rows, d), lambda j: (0, 0)),
            pl.BlockSpec((d, tn), lambda j: (0, j)),
            pl.BlockSpec((1, tn), lambda j: (0, j)),
        ],
        out_specs=pl.BlockSpec((rows, tn), lambda j: (0, j)),
        compiler_params=_params(("arbitrary",), 2 * d * tn * 4 + 3 * d * tn * 2),
        name="ada_mod",
    )(c_pad, ada_w, ada_b.reshape(1, n))
    return out[:bsz]


def _norm_mod_kernel(x_ref, g_ref, m_ref, o_ref, *, shift_row, scale_row):
    x = x_ref[...]
    y = x * lax.rsqrt(jnp.mean(x * x, axis=-1, keepdims=True) + EPS) * g_ref[...]
    shift = m_ref[0, shift_row:shift_row + 1, :]
    scale = m_ref[0, scale_row:scale_row + 1, :]
    o_ref[...] = (y * (1.0 + scale) + shift).astype(o_ref.dtype)


def _norm_mod(x2, gain, m, shift_row, scale_row, seq):
    rows, d = x2.shape
    tr = 256
    per_b = seq // tr
    return pl.pallas_call(
        functools.partial(_norm_mod_kernel, shift_row=shift_row, scale_row=scale_row),
        out_shape=jax.ShapeDtypeStruct((rows, d), BF16),
        grid=(rows // tr,),
        in_specs=[
            pl.BlockSpec((tr, d), lambda i: (i, 0)),
            pl.BlockSpec((1, d), lambda i: (0, 0)),
            pl.BlockSpec((1, N_MOD, d), lambda i: (i // per_b, 0, 0)),
        ],
        out_specs=pl.BlockSpec((tr, d), lambda i: (i, 0)),
        compiler_params=_params(("parallel",), 2 * tr * d * 6 + 4 * tr * d * 4),
        name="norm_mod",
    )(x2, gain.reshape(1, d), m)


def _resid_norm_kernel(x_ref, y_ref, g_ref, m_ref, o_ref, *, gate_row):
    y = y_ref[...]
    yn = y * lax.rsqrt(jnp.mean(y * y, axis=-1, keepdims=True) + EPS) * g_ref[...]
    gate = m_ref[0, gate_row:gate_row + 1, :]
    o_ref[...] = x_ref[...] + gate * yn


def _resid_norm(x2, y2, gain, m, gate_row, seq):
    rows, d = x2.shape
    tr = 256
    per_b = seq // tr
    return pl.pallas_call(
        functools.partial(_resid_norm_kernel, gate_row=gate_row),
        out_shape=jax.ShapeDtypeStruct((rows, d), F32),
        grid=(rows // tr,),
        in_specs=[
            pl.BlockSpec((tr, d), lambda i: (i, 0)),
            pl.BlockSpec((tr, d), lambda i: (i, 0)),
            pl.BlockSpec((1, d), lambda i: (0, 0)),
            pl.BlockSpec((1, N_MOD, d), lambda i: (i // per_b, 0, 0)),
        ],
        out_specs=pl.BlockSpec((tr, d), lambda i: (i, 0)),
        compiler_params=_params(("parallel",), 2 * tr * d * 12 + 4 * tr * d * 4),
        name="resid_norm",
    )(x2, y2, gain.reshape(1, d), m)


def _resid_norm_mod_kernel(x_ref, y_ref, g1_ref, g2_ref, ma_ref, mb_ref, xo_ref, ho_ref, *,
                           gate_row, shift_row, scale_row):
    y = y_ref[...]
    yn = y * lax.rsqrt(jnp.mean(y * y, axis=-1, keepdims=True) + EPS) * g1_ref[...]
    x = x_ref[...] + ma_ref[0, gate_row:gate_row + 1, :] * yn
    xo_ref[...] = x
    h = x * lax.rsqrt(jnp.mean(x * x, axis=-1, keepdims=True) + EPS) * g2_ref[...]
    shift = mb_ref[0, shift_row:shift_row + 1, :]
    scale = mb_ref[0, scale_row:scale_row + 1, :]
    ho_ref[...] = (h * (1.0 + scale) + shift).astype(ho_ref.dtype)


def _resid_norm_mod(x2, y2, gain1, gain2, m_gate, m_mod, gate_row, shift_row, scale_row, seq):
    rows, d = x2.shape
    tr = 256
    per_b = seq // tr
    row_spec = pl.BlockSpec((tr, d), lambda i: (i, 0))
    vec_spec = pl.BlockSpec((1, d), lambda i: (0, 0))
    mod_spec = pl.BlockSpec((1, N_MOD, d), lambda i: (i // per_b, 0, 0))
    return pl.pallas_call(
        functools.partial(_resid_norm_mod_kernel, gate_row=gate_row, shift_row=shift_row,
                          scale_row=scale_row),
        out_shape=(jax.ShapeDtypeStruct((rows, d), F32), jax.ShapeDtypeStruct((rows, d), BF16)),
        grid=(rows // tr,),
        in_specs=[row_spec, row_spec, vec_spec, vec_spec, mod_spec, mod_spec],
        out_specs=(row_spec, row_spec),
        compiler_params=_params(("parallel",), 2 * tr * d * 14 + 5 * tr * d * 4),
        name="resid_norm_mod",
    )(x2, y2, gain1.reshape(1, d), gain2.reshape(1, d), m_gate, m_mod)


def _swap_half_pairs(y):
    lane = lax.broadcasted_iota(jnp.int32, y.shape, 1)
    fwd = pltpu.roll(y, HEAD_DIM - 32, axis=1)
    bwd = pltpu.roll(y, 32, axis=1)
    return jnp.where((lane % 64) < 32, fwd, bwd)


def _qkv_kernel(a_ref, w_ref, cos_ref, sin_ref, qg_ref, kg_ref, o_ref, *,
                n_q_blocks, gq_lo, gq_hi, gk_lo, gk_hi, heads_per_block):
    j = pl.program_id(1)
    acc = jnp.dot(a_ref[...], w_ref[...], preferred_element_type=F32)
    is_q = j < n_q_blocks
    is_gq = (j >= gq_lo) & (j < gq_hi)
    is_gk = (j >= gk_lo) & (j < gk_hi)
    rope = is_gq | is_gk
    out_scale = jnp.where(is_q, ATTN_SCALE * LOG2E, 1.0).astype(F32)

    @pl.when(rope)
    def _():
        gain = jnp.where(is_gq, qg_ref[...], kg_ref[...])
        cos = cos_ref[...]
        sin = sin_ref[...]
        for h in range(heads_per_block):
            xh = acc[:, h * HEAD_DIM:(h + 1) * HEAD_DIM]
            y = xh * lax.rsqrt(jnp.mean(xh * xh, axis=-1, keepdims=True) + EPS) * gain
            y = y * cos + _swap_half_pairs(y) * sin
            o_ref[:, h * HEAD_DIM:(h + 1) * HEAD_DIM] = (y * out_scale).astype(o_ref.dtype)

    @pl.when(jnp.logical_not(rope))
    def _():
        o_ref[...] = (acc * out_scale).astype(o_ref.dtype)


def _qkv_proj(h2, w, layer, cos_t, sin_t, q_gain, k_gain, cfg, seq):
    m, k = h2.shape
    n = w.shape[2]
    tm, tn = 1024, 512
    hpb = tn // HEAD_DIM
    blk = lambda head: head // hpb
    kern = functools.partial(
        _qkv_kernel,
        n_q_blocks=blk(cfg["n_heads"]),
        gq_lo=blk(cfg["q_gqa0"]), gq_hi=blk(cfg["n_heads"]),
        gk_lo=blk(cfg["k_gqa0"]), gk_hi=blk(cfg["v0"]),
        heads_per_block=hpb)
    per_b = seq // tm
    return pl.pallas_call(
        kern,
        out_shape=jax.ShapeDtypeStruct((m, n), BF16),
        grid=(m // tm, n // tn),
        in_specs=[
            pl.BlockSpec((tm, k), lambda i, j: (i, 0)),
            pl.BlockSpec((None, k, tn), lambda i, j: (layer, 0, j)),
            pl.BlockSpec((tm, HEAD_DIM), lambda i, j: (i % per_b, 0)),
            pl.BlockSpec((tm, HEAD_DIM), lambda i, j: (i % per_b, 0)),
            pl.BlockSpec((1, HEAD_DIM), lambda i, j: (0, 0)),
            pl.BlockSpec((1, HEAD_DIM), lambda i, j: (0, 0)),
        ],
        out_specs=pl.BlockSpec((tm, tn), lambda i, j: (i, j)),
        compiler_params=_params(
            ("parallel", "arbitrary"),
            2 * (tm * k * 2 + k * tn * 2 + tm * tn * 2 + 2 * tm * HEAD_DIM * 4) + 3 * tm * tn * 4),
        name="qkv_proj",
    )(h2, w, cos_t, sin_t, q_gain.reshape(1, HEAD_DIM), k_gain.reshape(1, HEAD_DIM))


def _out_proj_kernel(a0_ref, a1_ref, a2_ref, w_ref, o_ref, *, splits):
    s0, s1, s2 = splits
    acc = jnp.dot(a0_ref[...], w_ref[0:s0, :], preferred_element_type=F32)
    acc += jnp.dot(a1_ref[...], w_ref[s0:s0 + s1, :], preferred_element_type=F32)
    acc += jnp.dot(a2_ref[...], w_ref[s0 + s1:s0 + s1 + s2, :], preferred_element_type=F32)
    o_ref[...] = acc


def _out_proj(o_na, o_dil, o_gqa, w, layer):
    m = o_na.shape[0]
    _, k, n = w.shape
    splits = (o_na.shape[1], o_dil.shape[1], o_gqa.shape[1])
    tm, tn = 1024, 1024
    return pl.pallas_call(
        functools.partial(_out_proj_kernel, splits=splits),
        out_shape=jax.ShapeDtypeStruct((m, n), F32),
        grid=(m // tm, n // tn),
        in_specs=[
            pl.BlockSpec((tm, splits[0]), lambda i, j: (i, 0)),
            pl.BlockSpec((tm, splits[1]), lambda i, j: (i, 0)),
            pl.BlockSpec((tm, splits[2]), lambda i, j: (i, 0)),
            pl.BlockSpec((None, k, tn), lambda i, j: (layer, 0, j)),
        ],
        out_specs=pl.BlockSpec((tm, tn), lambda i, j: (i, j)),
        compiler_params=_params(
            ("parallel", "arbitrary"),
            2 * (tm * k * 2 + k * tn * 2 + tm * tn * 4) + 2 * tm * tn * 4),
        name="out_proj",
    )(o_na, o_dil, o_gqa, w)


def _mlp_in_kernel(a_ref, w_ref, o_ref):
    acc = jnp.dot(a_ref[...], w_ref[...], preferred_element_type=F32)
    r = jnp.maximum(acc, 0.0)
    o_ref[...] = (r * r).astype(o_ref.dtype)


def _mlp_in(h2, w, layer):
    m, k = h2.shape
    n = w.shape[2]
    tm, tn = 1024, 1024
    return pl.pallas_call(
        _mlp_in_kernel,
        out_shape=jax.ShapeDtypeStruct((m, n), BF16),
        grid=(m // tm, n // tn),
        in_specs=[
            pl.BlockSpec((tm, k), lambda i, j: (i, 0)),
            pl.BlockSpec((None, k, tn), lambda i, j: (layer, 0, j)),
        ],
        out_specs=pl.BlockSpec((tm, tn), lambda i, j: (i, j)),
        compiler_params=_params(
            ("parallel", "arbitrary"),
            2 * (tm * k * 2 + k * tn * 2 + tm * tn * 2) + 2 * tm * tn * 4),
        name="mlp_in",
    )(h2, w)


def _mlp_out_kernel(a_ref, w_ref, o_ref):
    kk = pl.program_id(2)
    acc = jnp.dot(a_ref[...], w_ref[...], preferred_element_type=F32)

    @pl.when(kk == 0)
    def _():
        o_ref[...] = acc

    @pl.when(kk != 0)
    def _():
        o_ref[...] += acc


def _mlp_out(u2, w, layer):
    m, k = u2.shape
    n = w.shape[2]
    tm, tn, tk = 1024, 1024, 4096
    return pl.pallas_call(
        _mlp_out_kernel,
        out_shape=jax.ShapeDtypeStruct((m, n), F32),
        grid=(m // tm, n // tn, k // tk),
        in_specs=[
            pl.BlockSpec((tm, tk), lambda i, j, kk: (i, kk)),
            pl.BlockSpec((None, tk, tn), lambda i, j, kk: (layer, kk, j)),
        ],
        out_specs=pl.BlockSpec((tm, tn), lambda i, j, kk: (i, j)),
        compiler_params=_params(
            ("parallel", "arbitrary", "arbitrary"),
            2 * (tm * tk * 2 + tk * tn * 2 + tm * tn * 4) + 2 * tm * tn * 4),
        name="mlp_out",
    )(u2, w)


def _qk_scores(q, k):
    return lax.dot_general(q, k, (((1,), (1,)), ((), ())), preferred_element_type=F32)


def _toeplitz_rows(row, n_rows):
    x = jnp.broadcast_to(row, (n_rows, row.shape[-1]))
    return pltpu.roll(x, 0, axis=1, stride=1, stride_axis=0)


def _na_kernel(q_ref, k_ref, v_ref, w_ref, cm_ref, o_ref, bias_ref, *, grid_rows, q_rows, k_rows,
               n_blk):
    i = pl.program_id(2)

    @pl.when((i == 0) | (i == 1) | (i == n_blk - 1))
    def _():
        for a in range(q_rows):
            band = _toeplitz_rows(w_ref[0, 0, a], GRID_W)
            bias_ref[a * GRID_W:(a + 1) * GRID_W, :] = band + cm_ref[...]

    kb = jnp.clip(i * q_rows - NA_ROWS // 2, 0, grid_rows - k_rows) * GRID_W
    kb = pl.multiple_of(kb, 256)
    q = q_ref[...]
    kw = k_ref[pl.ds(kb, k_rows * GRID_W), :]
    vw = v_ref[pl.ds(kb, k_rows * GRID_W), :]
    s = _qk_scores(q, kw) + bias_ref[...]
    m = jnp.max(s, axis=-1, keepdims=True)
    p = jnp.exp2(s - m)
    l = jnp.sum(p, axis=-1, keepdims=True)
    o = jnp.dot(p.astype(BF16), vw, preferred_element_type=F32)
    o_ref[...] = (o / l).astype(o_ref.dtype)


def _na_tables(rpb, grid_rows, q_rows, k_rows):
    n_blk = grid_rows // q_rows
    width = k_rows * GRID_W
    blocks = np.array([0, 1, n_blk - 1])
    a = np.arange(q_rows)
    r = blocks[:, None] * q_rows + a[None, :]
    kb = np.clip(blocks * q_rows - NA_ROWS // 2, 0, grid_rows - k_rows)
    rs = np.clip(r - NA_ROWS // 2, 0, grid_rows - NA_ROWS)
    j = np.arange(width)
    krl = ((j + GRID_W // 2) // GRID_W) % k_rows
    e = (j + GRID_W // 2) % GRID_W - GRID_W // 2
    kr = kb[:, None, None] + krl[None, None, :]
    row_ok = (kr >= rs[:, :, None]) & (kr < rs[:, :, None] + NA_ROWS)
    roff = kr - r[:, :, None] + NA_ROWS - 1
    ok = row_ok & (np.abs(e) <= NA_COLS - 1)[None, None, :]
    n_c = 2 * NA_COLS - 1
    idx = np.where(ok, roff * n_c + (e + NA_COLS - 1)[None, None, :], 0)
    flat = rpb.reshape(-1, rpb.shape[-2] * rpb.shape[-1])
    vals = jnp.take(flat, jnp.asarray(idx.reshape(-1), jnp.int32), axis=1)
    vals = vals.reshape(flat.shape[0], 3, q_rows, 1, width)
    tab = jnp.where(jnp.asarray(ok)[None, :, :, None, :], vals * LOG2E, NEG_INF)
    col = np.arange(GRID_W)
    cs = np.clip(col - NA_COLS // 2, 0, GRID_W - NA_COLS)
    col_ok = (col[None, :] >= cs[:, None]) & (col[None, :] < cs[:, None] + NA_COLS)
    cmask = np.where(np.tile(col_ok, (1, k_rows)), 0.0, NEG_INF).astype(np.float32)
    return tab, jnp.asarray(cmask)


def _na_attention(qkv, tab, cmask, layer, cfg, bsz, seq):
    grid_rows = seq // GRID_W
    q_rows, k_rows = tab.shape[2], cmask.shape[1] // GRID_W
    tq, tkw = q_rows * GRID_W, k_rows * GRID_W
    n_blk = grid_rows // q_rows
    assert n_blk >= 3
    heads = cfg["heads_na"]
    k0, v0 = cfg["k0"], cfg["v0"]

    def btype(i):
        return jnp.where(i == 0, 0, jnp.where(i == n_blk - 1, 2, 1))

    kern = functools.partial(_na_kernel, grid_rows=grid_rows, q_rows=q_rows, k_rows=k_rows,
                             n_blk=n_blk)
    return pl.pallas_call(
        kern,
        out_shape=jax.ShapeDtypeStruct((bsz * seq, heads * HEAD_DIM), BF16),
        grid=(bsz, heads, n_blk),
        in_specs=[
            pl.BlockSpec((tq, HEAD_DIM), lambda b, h, i: (b * n_blk + i, h)),
            pl.BlockSpec((seq, HEAD_DIM), lambda b, h, i: (b, k0 + h)),
            pl.BlockSpec((seq, HEAD_DIM), lambda b, h, i: (b, v0 + h)),
            pl.BlockSpec((1, 1, q_rows, 1, tkw),
                         lambda b, h, i: (layer * heads + h, btype(i), 0, 0, 0)),
            pl.BlockSpec((GRID_W, tkw), lambda b, h, i: (0, 0)),
        ],
        out_specs=pl.BlockSpec((tq, HEAD_DIM), lambda b, h, i: (b * n_blk + i, h)),
        scratch_shapes=[pltpu.VMEM((tq, tkw), F32)],
        compiler_params=_params(
            ("parallel", "parallel", "arbitrary"),
            2 * (2 * seq * HEAD_DIM * 2 + 2 * tq * HEAD_DIM * 2) + 6 * tq * tkw * 4),
        name="na_attn",
    )(qkv, qkv, qkv, tab, cmask)


def _t5_bucket_np(rel):
    nb = T5_BUCKETS // 2
    max_exact = nb // 2
    ret = np.where(rel > 0, nb, 0)
    n = np.abs(rel)
    nf = np.maximum(n, 1).astype(np.float64)
    large = max_exact + (np.log(nf / max_exact) / math.log(T5_MAX_DIST / max_exact)
                         * (nb - max_exact)).astype(np.int64)
    large = np.minimum(large, nb - 1)
    return ret + np.where(n < max_exact, n, large)


def _dil_tables(t5_table, tq, n_side):
    n_chunks = 2 * n_side + 1
    rel = ((np.arange(n_chunks) - n_side + 1)[:, None] * tq - np.arange(2 * tq)[None, :]).reshape(-1)
    count = np.zeros(rel.shape, np.int64)
    for window, d in DIL_PATTERNS:
        half = window // (2 * d)
        count += ((rel % d) == 0) & (np.abs(rel) <= half * d)
    bucket = _t5_bucket_np(rel)
    logc = np.log(np.maximum(count, 1)).astype(np.float32)
    f = (t5_table.T[:, bucket] + jnp.asarray(logc)[None, :]) * LOG2E
    f = jnp.where(jnp.asarray(count > 0)[None, :], f, NEG_INF)
    return f.reshape(t5_table.shape[1], n_chunks, 1, 2 * tq)


def _dil_kernel(q_ref, k_ref, v_ref, w_ref, o_ref, bias_ref, *, tq, n_side, n_blk):
    i = pl.program_id(2)

    @pl.when(i == 0)
    def _():
        for c in range(2 * n_side + 1):
            bias_ref[c] = _toeplitz_rows(w_ref[0, c], tq)[:, tq:2 * tq]

    q = q_ref[...]

    def start_of(c):
        blk = jnp.clip(i + c, 0, n_blk - 1)
        return pl.multiple_of(blk * tq, tq)

    def scores_t(c):
        return lax.dot_general(k_ref[pl.ds(start_of(c), tq), :], q, (((1,), (1,)), ((), ())),
                               preferred_element_type=F32) + bias_ref[c + n_side]

    def pv_t(c, pt):
        return lax.dot_general(v_ref[pl.ds(start_of(c), tq), :], pt.astype(BF16),
                               (((0,), (0,)), ((), ())), preferred_element_type=F32)

    order = [0] + [c for c in range(-n_side, n_side + 1) if c != 0]
    st = scores_t(0)
    m = l = acc = None
    for n, c in enumerate(order):
        st_next = scores_t(order[n + 1]) if n + 1 < len(order) else None
        m_cur = jnp.max(st, axis=0, keepdims=True)
        if n == 0:
            m = m_cur
            pt = jnp.exp2(st - m)
            l = jnp.sum(pt, axis=0, keepdims=True)
            acc = pv_t(c, pt)
        else:
            off = jnp.where((i + c >= 0) & (i + c < n_blk), 0.0, NEG_INF).astype(F32)
            m_new = jnp.maximum(m, m_cur + off)
            alpha = jnp.exp2(m - m_new)
            pt = jnp.exp2(st - (m_new - off))
            l = alpha * l + jnp.sum(pt, axis=0, keepdims=True)
            acc = alpha * acc + pv_t(c, pt)
            m = m_new
        st = st_next
    o_ref[...] = (acc / l).T.astype(o_ref.dtype)


def _dil_attention(qkv, tab, cfg, bsz, seq):
    tq = tab.shape[-1] // 2
    n_chunks = tab.shape[1]
    n_side = (n_chunks - 1) // 2
    n_blk = seq // tq
    heads = cfg["heads_dil"]
    q0 = cfg["heads_na"]
    k0, v0 = cfg["k0"] + cfg["heads_na"], cfg["v0"] + cfg["heads_na"]
    kern = functools.partial(_dil_kernel, tq=tq, n_side=n_side, n_blk=n_blk)
    return pl.pallas_call(
        kern,
        out_shape=jax.ShapeDtypeStruct((bsz * seq, heads * HEAD_DIM), BF16),
        grid=(bsz, heads, n_blk),
        in_specs=[
            pl.BlockSpec((tq, HEAD_DIM), lambda b, h, i: (b * n_blk + i, q0 + h)),
            pl.BlockSpec((seq, HEAD_DIM), lambda b, h, i: (b, k0 + h)),
            pl.BlockSpec((seq, HEAD_DIM), lambda b, h, i: (b, v0 + h)),
            pl.BlockSpec((1, n_chunks, 1, 2 * tq), lambda b, h, i: (h, 0, 0, 0)),
        ],
        out_specs=pl.BlockSpec((tq, HEAD_DIM), lambda b, h, i: (b * n_blk + i, h)),
        scratch_shapes=[pltpu.VMEM((n_chunks, tq, tq), F32)],
        compiler_params=_params(
            ("parallel", "parallel", "arbitrary"),
            2 * (2 * seq * HEAD_DIM * 2 + 2 * tq * HEAD_DIM * 2) + n_chunks * tq * tq * 4
            + 2 * tq * 2 * tq * 4 + 6 * tq * tq * 4),
        name="dil_attn",
    )(qkv, qkv, qkv, tab)


def _gqa_kernel(q0_ref, q1_ref, q2_ref, k_ref, v_ref, o_ref, acc_ref, sa_ref, sb_ref, *,
                tq, tk, seq):
    q = jnp.concatenate([q0_ref[...], q1_ref[...], q2_ref[...]], axis=0)
    mq = 3 * tq
    n = seq // tk
    acc_ref[...] = jnp.zeros_like(acc_ref)
    bufs = (sa_ref, sb_ref)

    def scores(j):
        return lax.dot_general(k_ref[j * tk:(j + 1) * tk, :], q, (((1,), (1,)), ((), ())),
                               preferred_element_type=F32)

    bufs[0][...] = scores(0)
    m = jnp.full((1, mq), NEG_INF, F32)
    l = jnp.zeros((1, mq), F32)
    for j in range(n):
        if j + 1 < n:
            bufs[(j + 1) % 2][...] = scores(j + 1)
        st = bufs[j % 2][...]
        m_new = jnp.maximum(m, jnp.max(st, axis=0, keepdims=True))
        alpha = jnp.exp2(m - m_new)
        pt = jnp.exp2(st - m_new)
        l = alpha * l + jnp.sum(pt, axis=0, keepdims=True)
        pv = lax.dot_general(v_ref[j * tk:(j + 1) * tk, :], pt.astype(BF16),
                             (((0,), (0,)), ((), ())), preferred_element_type=F32)
        acc_ref[...] = alpha * acc_ref[...] + pv
        m = m_new
    o = (acc_ref[...] / l).T
    for g in range(3):
        o_ref[:, g * HEAD_DIM:(g + 1) * HEAD_DIM] = o[g * tq:(g + 1) * tq].astype(o_ref.dtype)


def _gqa_attention(qkv, cfg, bsz, seq):
    tq, tk = 256, 512
    assert seq % tk == 0
    n_blk = seq // tq
    group = cfg["gqa_group"]
    assert group == 3
    kv_heads = cfg["kv_heads_gqa"]
    heads = kv_heads * group
    q0 = cfg["q_gqa0"]
    k0, v0 = cfg["k_gqa0"], cfg["v_gqa0"]
    kern = functools.partial(_gqa_kernel, tq=tq, tk=tk, seq=seq)
    q_spec = lambda g: pl.BlockSpec(
        (tq, HEAD_DIM), lambda b, kh, i: (b * n_blk + i, q0 + kh * group + g))
    return pl.pallas_call(
        kern,
        out_shape=jax.ShapeDtypeStruct((bsz * seq, heads * HEAD_DIM), BF16),
        grid=(bsz, kv_heads, n_blk),
        in_specs=[q_spec(0), q_spec(1), q_spec(2),
                  pl.BlockSpec((seq, HEAD_DIM), lambda b, kh, i: (b, k0 + kh)),
                  pl.BlockSpec((seq, HEAD_DIM), lambda b, kh, i: (b, v0 + kh))],
        out_specs=pl.BlockSpec((tq, group * HEAD_DIM), lambda b, kh, i: (b * n_blk + i, kh)),
        scratch_shapes=[pltpu.VMEM((HEAD_DIM, group * tq), F32),
                        pltpu.VMEM((tk, group * tq), F32),
                        pltpu.VMEM((tk, group * tq), F32)],
        compiler_params=_params(
            ("parallel", "parallel", "arbitrary"),
            2 * (2 * seq * HEAD_DIM * 2 + 12 * tq * HEAD_DIM * 2) + 8 * group * tq * tk * 4),
        name="gqa_attn",
    )(qkv, qkv, qkv, qkv, qkv)


def _rope_tables(seq):
    half = HEAD_DIM // 2
    inv = np.exp(-math.log(ROPE_THETA) * np.arange(0, half, 2, dtype=np.float64) / half)
    t = np.arange(seq)
    ang_r = (t // GRID_W)[:, None] * inv[None, :]
    ang_c = (t % GRID_W)[:, None] * inv[None, :]
    cos = np.concatenate([np.cos(ang_r)] * 2 + [np.cos(ang_c)] * 2, axis=-1)
    sin = np.concatenate([-np.sin(ang_r), np.sin(ang_r), -np.sin(ang_c), np.sin(ang_c)], axis=-1)
    return jnp.asarray(cos, F32), jnp.asarray(sin, F32)


def _head_config(d_model, na_rpb, t5_table):
    n_heads = d_model // HEAD_DIM
    heads_na = na_rpb.shape[1]
    heads_dil = t5_table.shape[1]
    heads_gqa = n_heads - heads_na - heads_dil
    kv_heads_gqa = heads_gqa // 3
    k0 = n_heads
    v0 = k0 + heads_na + heads_dil + kv_heads_gqa
    return dict(
        n_heads=n_heads, heads_na=heads_na, heads_dil=heads_dil, gqa_group=3,
        kv_heads_gqa=kv_heads_gqa, q_gqa0=heads_na + heads_dil,
        k0=k0, k_gqa0=k0 + heads_na + heads_dil, v0=v0, v_gqa0=v0 + heads_na + heads_dil)


def kernel(x, c, ada_w, ada_b, ada_layer_emb, norm_gains, w_in, w_out, q_gain, k_gain, na_rpb,
           t5_table, w_mlp_in, w_mlp_out):
    bsz, seq, d = x.shape
    depth = w_in.shape[0]
    cfg = _head_config(d, na_rpb, t5_table)
    grid_rows = seq // GRID_W

    mod = _ada_mod(c, ada_w, ada_b).reshape(bsz, N_MOD, d)
    mods = [mod + ada_layer_emb[l][None] for l in range(depth)]
    cos_t, sin_t = _rope_tables(seq)
    dil_tab = _dil_tables(t5_table, tq=512, n_side=2)
    na_tab, na_cmask = _na_tables(na_rpb, grid_rows, 8, 16)
    w_in, w_out = w_in.astype(BF16), w_out.astype(BF16)
    w_mlp_in, w_mlp_out = w_mlp_in.astype(BF16), w_mlp_out.astype(BF16)

    x2 = x.reshape(bsz * seq, d)
    h = _norm_mod(x2, norm_gains[0, 0], mods[0], 0, 1, seq)
    for l in range(depth):
        m = mods[l]
        qkv = _qkv_proj(h, w_in, l, cos_t, sin_t, q_gain[l], k_gain[l], cfg, seq)
        o_na = _na_attention(qkv, na_tab, na_cmask, l, cfg, bsz, seq)
        o_dil = _dil_attention(qkv, dil_tab, cfg, bsz, seq)
        o_gqa = _gqa_attention(qkv, cfg, bsz, seq)
        y = _out_proj(o_na, o_dil, o_gqa, w_out, l)
        x2, h = _resid_norm_mod(x2, y, norm_gains[l, 1], norm_gains[l, 2], m, m, 2, 3, 4, seq)
        u = _mlp_in(h, w_mlp_in, l)
        y = _mlp_out(u, w_mlp_out, l)
        if l + 1 < depth:
            x2, h = _resid_norm_mod(x2, y, norm_gains[l, 3], norm_gains[l + 1, 0], m, mods[l + 1],
                                    5, 0, 1, seq)
        else:
            x2 = _resid_norm(x2, y, norm_gains[l, 3], m, 5, seq)
    return x2.reshape(bsz, seq, d)
```

```python
import functools
import math

import numpy as np
import jax
import jax.numpy as jnp
from jax import lax
from jax.experimental import pallas as pl
from jax.experimental.pallas import tpu as pltpu

HEAD_DIM = 128
GRID_W = 64
NA_ROWS = 8
NA_COLS = 16
DIL_PATTERNS = ((128, 1), (512, 4), (2048, 16))
T5_BUCKETS = 32
T5_MAX_DIST = 1024
ROPE_THETA = 10000.0
N_MOD = 6
EPS = 1e-6
NEG_INF = -1e30
ATTN_SCALE = HEAD_DIM ** -0.5
LOG2E = math.log2(math.e)

V7X_VMEM_BYTES = 64 * 1024 * 1024
V7X_LANES = 128

BF16 = jnp.bfloat16
F32 = jnp.float32


def _params(semantics, vmem_bytes):
    limit = min(int(vmem_bytes * 1.25) + (4 << 20), V7X_VMEM_BYTES - (6 << 20))
    return pltpu.CompilerParams(dimension_semantics=semantics, vmem_limit_bytes=limit)


def _ada_kernel(c_ref, w_ref, b_ref, o_ref):
    c = c_ref[...]
    a = c * jax.nn.sigmoid(c)
    a_hi = a.astype(BF16)
    a_lo = (a - a_hi.astype(F32)).astype(BF16)
    w = w_ref[...]
    w_hi = w.astype(BF16)
    w_lo = (w - w_hi.astype(F32)).astype(BF16)
    acc = jnp.dot(a_hi, w_hi, preferred_element_type=F32)
    acc += jnp.dot(a_hi, w_lo, preferred_element_type=F32)
    acc += jnp.dot(a_lo, w_hi, preferred_element_type=F32)
    o_ref[...] = acc + b_ref[...]


def _ada_mod(c, ada_w, ada_b):
    bsz, d = c.shape
    n = ada_w.shape[1]
    rows = 8
    tn = 512
    c_pad = jnp.zeros((rows, d), F32).at[:bsz].set(c)
    out = pl.pallas_call(
        _ada_kernel,
        out_shape=jax.ShapeDtypeStruct((rows, n), F32),
        grid=(n // tn,),
        in_specs=[
            pl.BlockSpec((rows, d), lambda j: (0, 0)),
            pl.BlockSpec((d, tn), lambda j: (0, j)),
            pl.BlockSpec((1, tn), lambda j: (0, j)),
        ],
        out_specs=pl.BlockSpec((rows, tn), lambda j: (0, j)),
        compiler_params=_params(("arbitrary",), 2 * d * tn * 4 + 3 * d * tn * 2),
        name="ada_mod",
    )(c_pad, ada_w, ada_b.reshape(1, n))
    return out[:bsz]


def _norm_mod_kernel(x_ref, g_ref, m_ref, o_ref, *, shift_row, scale_row):
    x = x_ref[...]
    y = x * lax.rsqrt(jnp.mean(x * x, axis=-1, keepdims=True) + EPS) * g_ref[...]
    shift = m_ref[0, shift_row:shift_row + 1, :]
    scale = m_ref[0, scale_row:scale_row + 1, :]
    o_ref[...] = (y * (1.0 + scale) + shift).astype(o_ref.dtype)


def _norm_mod(x2, gain, m, shift_row, scale_row, seq):
    rows, d = x2.shape
    tr = 256
    per_b = seq // tr
    return pl.pallas_call(
        functools.partial(_norm_mod_kernel, shift_row=shift_row, scale_row=scale_row),
        out_shape=jax.ShapeDtypeStruct((rows, d), BF16),
        grid=(rows // tr,),
        in_specs=[
            pl.BlockSpec((tr, d), lambda i: (i, 0)),
            pl.BlockSpec((1, d), lambda i: (0, 0)),
            pl.BlockSpec((1, N_MOD, d), lambda i: (i // per_b, 0, 0)),
        ],
        out_specs=pl.BlockSpec((tr, d), lambda i: (i, 0)),
        compiler_params=_params(("parallel",), 2 * tr * d * 6 + 4 * tr * d * 4),
        name="norm_mod",
    )(x2, gain.reshape(1, d), m)


def _resid_norm_kernel(x_ref, y_ref, g_ref, m_ref, o_ref, *, gate_row):
    y = y_ref[...]
    yn = y * lax.rsqrt(jnp.mean(y * y, axis=-1, keepdims=True) + EPS) * g_ref[...]
    gate = m_ref[0, gate_row:gate_row + 1, :]
    o_ref[...] = x_ref[...] + gate * yn


def _resid_norm(x2, y2, gain, m, gate_row, seq):
    rows, d = x2.shape
    tr = 256
    per_b = seq // tr
    return pl.pallas_call(
        functools.partial(_resid_norm_kernel, gate_row=gate_row),
        out_shape=jax.ShapeDtypeStruct((rows, d), F32),
        grid=(rows // tr,),
        in_specs=[
            pl.BlockSpec((tr, d), lambda i: (i, 0)),
            pl.BlockSpec((tr, d), lambda i: (i, 0)),
            pl.BlockSpec((1, d), lambda i: (0, 0)),
            pl.BlockSpec((1, N_MOD, d), lambda i: (i // per_b, 0, 0)),
        ],
        out_specs=pl.BlockSpec((tr, d), lambda i: (i, 0)),
        compiler_params=_params(("parallel",), 2 * tr * d * 12 + 4 * tr * d * 4),
        name="resid_norm",
    )(x2, y2, gain.reshape(1, d), m)


def _resid_norm_mod_kernel(x_ref, y_ref, g1_ref, g2_ref, ma_ref, mb_ref, xo_ref, ho_ref, *,
                           gate_row, shift_row, scale_row):
    y = y_ref[...]
    yn = y * lax.rsqrt(jnp.mean(y * y, axis=-1, keepdims=True) + EPS) * g1_ref[...]
    x = x_ref[...] + ma_ref[0, gate_row:gate_row + 1, :] * yn
    xo_ref[...] = x
    h = x * lax.rsqrt(jnp.mean(x * x, axis=-1, keepdims=True) + EPS) * g2_ref[...]
    shift = mb_ref[0, shift_row:shift_row + 1, :]
    scale = mb_ref[0, scale_row:scale_row + 1, :]
    ho_ref[...] = (h * (1.0 + scale) + shift).astype(ho_ref.dtype)


def _resid_norm_mod(x2, y2, gain1, gain2, m_gate, m_mod, gate_row, shift_row, scale_row, seq):
    rows, d = x2.shape
    tr = 256
    per_b = seq // tr
    row_spec = pl.BlockSpec((tr, d), lambda i: (i, 0))
    vec_spec = pl.BlockSpec((1, d), lambda i: (0, 0))
    mod_spec = pl.BlockSpec((1, N_MOD, d), lambda i: (i // per_b, 0, 0))
    return pl.pallas_call(
        functools.partial(_resid_norm_mod_kernel, gate_row=gate_row, shift_row=shift_row,
                          scale_row=scale_row),
        out_shape=(jax.ShapeDtypeStruct((rows, d), F32), jax.ShapeDtypeStruct((rows, d), BF16)),
        grid=(rows // tr,),
        in_specs=[row_spec, row_spec, vec_spec, vec_spec, mod_spec, mod_spec],
        out_specs=(row_spec, row_spec),
        compiler_params=_params(("parallel",), 2 * tr * d * 14 + 5 * tr * d * 4),
        name="resid_norm_mod",
    )(x2, y2, gain1.reshape(1, d), gain2.reshape(1, d), m_gate, m_mod)


def _swap_half_pairs(y):
    lane = lax.broadcasted_iota(jnp.int32, y.shape, 1)
    fwd = pltpu.roll(y, HEAD_DIM - 32, axis=1)
    bwd = pltpu.roll(y, 32, axis=1)
    return jnp.where((lane % 64) < 32, fwd, bwd)


def _qkv_kernel(a_ref, w_ref, cos_ref, sin_ref, qg_ref, kg_ref, o_ref, *,
                n_q_blocks, gq_lo, gq_hi, gk_lo, gk_hi, heads_per_block):
    j = pl.program_id(1)
    acc = jnp.dot(a_ref[...], w_ref[...], preferred_element_type=F32)
    is_q = j < n_q_blocks
    is_gq = (j >= gq_lo) & (j < gq_hi)
    is_gk = (j >= gk_lo) & (j < gk_hi)
    rope = is_gq | is_gk
    out_scale = jnp.where(is_q, ATTN_SCALE * LOG2E, 1.0).astype(F32)

    @pl.when(rope)
    def _():
        gain = jnp.where(is_gq, qg_ref[...], kg_ref[...])
        cos = cos_ref[...]
        sin = sin_ref[...]
        for h in range(heads_per_block):
            xh = acc[:, h * HEAD_DIM:(h + 1) * HEAD_DIM]
            y = xh * lax.rsqrt(jnp.mean(xh * xh, axis=-1, keepdims=True) + EPS) * gain
            y = y * cos + _swap_half_pairs(y) * sin
            o_ref[:, h * HEAD_DIM:(h + 1) * HEAD_DIM] = (y * out_scale).astype(o_ref.dtype)

    @pl.when(jnp.logical_not(rope))
    def _():
        o_ref[...] = (acc * out_scale).astype(o_ref.dtype)


def _qkv_proj(h2, w, layer, cos_t, sin_t, q_gain, k_gain, cfg, seq):
    m, k = h2.shape
    n = w.shape[2]
    tm, tn = 1024, 512
    hpb = tn // HEAD_DIM
    blk = lambda head: head // hpb
    kern = functools.partial(
        _qkv_kernel,
        n_q_blocks=blk(cfg["n_heads"]),
        gq_lo=blk(cfg["q_gqa0"]), gq_hi=blk(cfg["n_heads"]),
        gk_lo=blk(cfg["k_gqa0"]), gk_hi=blk(cfg["v0"]),
        heads_per_block=hpb)
    per_b = seq // tm
    return pl.pallas_call(
        kern,
        out_shape=jax.ShapeDtypeStruct((m, n), BF16),
        grid=(m // tm, n // tn),
        in_specs=[
            pl.BlockSpec((tm, k), lambda i, j: (i, 0)),
            pl.BlockSpec((None, k, tn), lambda i, j: (layer, 0, j)),
            pl.BlockSpec((tm, HEAD_DIM), lambda i, j: (i % per_b, 0)),
            pl.BlockSpec((tm, HEAD_DIM), lambda i, j: (i % per_b, 0)),
            pl.BlockSpec((1, HEAD_DIM), lambda i, j: (0, 0)),
            pl.BlockSpec((1, HEAD_DIM), lambda i, j: (0, 0)),
        ],
        out_specs=pl.BlockSpec((tm, tn), lambda i, j: (i, j)),
        compiler_params=_params(
            ("parallel", "arbitrary"),
            2 * (tm * k * 2 + k * tn * 2 + tm * tn * 2 + 2 * tm * HEAD_DIM * 4) + 3 * tm * tn * 4),
        name="qkv_proj",
    )(h2, w, cos_t, sin_t, q_gain.reshape(1, HEAD_DIM), k_gain.reshape(1, HEAD_DIM))


def _out_proj_kernel(a0_ref, a1_ref, a2_ref, w_ref, o_ref, *, splits):
    s0, s1, s2 = splits
    acc = jnp.dot(a0_ref[...], w_ref[0:s0, :], preferred_element_type=F32)
    acc += jnp.dot(a1_ref[...], w_ref[s0:s0 + s1, :], preferred_element_type=F32)
    acc += jnp.dot(a2_ref[...], w_ref[s0 + s1:s0 + s1 + s2, :], preferred_element_type=F32)
    o_ref[...] = acc


def _out_proj(o_na, o_dil, o_gqa, w, layer):
    m = o_na.shape[0]
    _, k, n = w.shape
    splits = (o_na.shape[1], o_dil.shape[1], o_gqa.shape[1])
    tm, tn = 1024, 1024
    return pl.pallas_call(
        functools.partial(_out_proj_kernel, splits=splits),
        out_shape=jax.ShapeDtypeStruct((m, n), F32),
        grid=(m // tm, n // tn),
        in_specs=[
            pl.BlockSpec((tm, splits[0]), lambda i, j: (i, 0)),
            pl.BlockSpec((tm, splits[1]), lambda i, j: (i, 0)),
            pl.BlockSpec((tm, splits[2]), lambda i, j: (i, 0)),
            pl.BlockSpec((None, k, tn), lambda i, j: (layer, 0, j)),
        ],
        out_specs=pl.BlockSpec((tm, tn), lambda i, j: (i, j)),
        compiler_params=_params(
            ("parallel", "arbitrary"),
            2 * (tm * k * 2 + k * tn * 2 + tm * tn * 4) + 2 * tm * tn * 4),
        name="out_proj",
    )(o_na, o_dil, o_gqa, w)


def _mlp_in_kernel(a_ref, w_ref, *refs, n_cast):
    src_refs, o_ref, dst_refs = refs[:n_cast], refs[n_cast], refs[n_cast + 1:]
    acc = jnp.dot(a_ref[...], w_ref[...], preferred_element_type=F32)
    r = jnp.maximum(acc, 0.0)
    o_ref[...] = (r * r).astype(o_ref.dtype)
    for src, dst in zip(src_refs, dst_refs):
        dst[...] = src[...].astype(dst.dtype)


def _mlp_in(h2, w, layer, cast_stacks=(), cast_layer=0):
    m, k = h2.shape
    n = w.shape[2]
    tm, tn = 1024, 1024
    grid = (m // tm, n // tn)
    steps = grid[0] * grid[1]
    cast_in, cast_out, cast_shapes, cast_bytes = [], [], [], 0
    for s in cast_stacks:
        _, rows, cols = s.shape
        slab = rows // steps
        assert slab * steps == rows and slab % 16 == 0
        cast_in.append(pl.BlockSpec(
            (None, slab, cols), lambda i, j: (cast_layer, i * grid[1] + j, 0)))
        cast_out.append(pl.BlockSpec((None, slab, cols), lambda i, j: (0, i * grid[1] + j, 0)))
        cast_shapes.append(jax.ShapeDtypeStruct((1, rows, cols), BF16))
        cast_bytes += slab * cols * 6
    outs = pl.pallas_call(
        functools.partial(_mlp_in_kernel, n_cast=len(cast_stacks)),
        out_shape=[jax.ShapeDtypeStruct((m, n), BF16)] + cast_shapes,
        grid=grid,
        in_specs=[
            pl.BlockSpec((tm, k), lambda i, j: (i, 0)),
            pl.BlockSpec((None, k, tn), lambda i, j: (layer, 0, j)),
        ] + cast_in,
        out_specs=[pl.BlockSpec((tm, tn), lambda i, j: (i, j))] + cast_out,
        compiler_params=_params(
            ("arbitrary", "arbitrary"),
            2 * (tm * k * 2 + k * tn * 2 + tm * tn * 2 + cast_bytes) + 2 * tm * tn * 4),
        name="mlp_in",
    )(h2, w, *cast_stacks)
    return outs[0], outs[1:]


def _mlp_out_kernel(a_ref, w_ref, o_ref):
    kk = pl.program_id(2)
    acc = jnp.dot(a_ref[...], w_ref[...], preferred_element_type=F32)

    @pl.when(kk == 0)
    def _():
        o_ref[...] = acc

    @pl.when(kk != 0)
    def _():
        o_ref[...] += acc


def _mlp_out(u2, w, layer):
    m, k = u2.shape
    n = w.shape[2]
    tm, tn, tk = 1024, 1024, 4096
    return pl.pallas_call(
        _mlp_out_kernel,
        out_shape=jax.ShapeDtypeStruct((m, n), F32),
        grid=(m // tm, n // tn, k // tk),
        in_specs=[
            pl.BlockSpec((tm, tk), lambda i, j, kk: (i, kk)),
            pl.BlockSpec((None, tk, tn), lambda i, j, kk: (layer, kk, j)),
        ],
        out_specs=pl.BlockSpec((tm, tn), lambda i, j, kk: (i, j)),
        compiler_params=_params(
            ("parallel", "arbitrary", "arbitrary"),
            2 * (tm * tk * 2 + tk * tn * 2 + tm * tn * 4) + 2 * tm * tn * 4),
        name="mlp_out",
    )(u2, w)


def _qk_scores(q, k):
    return lax.dot_general(q, k, (((1,), (1,)), ((), ())), preferred_element_type=F32)


def _toeplitz_rows(row, n_rows):
    x = jnp.broadcast_to(row, (n_rows, row.shape[-1]))
    return pltpu.roll(x, 0, axis=1, stride=1, stride_axis=0)


def _na_kernel(q_ref, k_ref, v_ref, w_ref, cm_ref, o_ref, bias_ref, *, grid_rows, q_rows, k_rows,
               n_blk):
    i = pl.program_id(2)

    @pl.when((i == 0) | (i == 1) | (i == n_blk - 1))
    def _():
        for a in range(q_rows):
            band = _toeplitz_rows(w_ref[0, 0, a], GRID_W)
            bias_ref[a * GRID_W:(a + 1) * GRID_W, :] = band + cm_ref[...]

    kb = jnp.clip(i * q_rows - NA_ROWS // 2, 0, grid_rows - k_rows) * GRID_W
    kb = pl.multiple_of(kb, 256)
    q = q_ref[...]
    kw = k_ref[pl.ds(kb, k_rows * GRID_W), :]
    vw = v_ref[pl.ds(kb, k_rows * GRID_W), :]
    s = _qk_scores(q, kw) + bias_ref[...]
    m = jnp.max(s, axis=-1, keepdims=True)
    p = jnp.exp2(s - m)
    l = jnp.sum(p, axis=-1, keepdims=True)
    o = jnp.dot(p.astype(BF16), vw, preferred_element_type=F32)
    o_ref[...] = (o / l).astype(o_ref.dtype)


def _na_tables(rpb, grid_rows, q_rows, k_rows):
    n_blk = grid_rows // q_rows
    width = k_rows * GRID_W
    blocks = np.array([0, 1, n_blk - 1])
    a = np.arange(q_rows)
    r = blocks[:, None] * q_rows + a[None, :]
    kb = np.clip(blocks * q_rows - NA_ROWS // 2, 0, grid_rows - k_rows)
    rs = np.clip(r - NA_ROWS // 2, 0, grid_rows - NA_ROWS)
    j = np.arange(width)
    krl = ((j + GRID_W // 2) // GRID_W) % k_rows
    e = (j + GRID_W // 2) % GRID_W - GRID_W // 2
    kr = kb[:, None, None] + krl[None, None, :]
    row_ok = (kr >= rs[:, :, None]) & (kr < rs[:, :, None] + NA_ROWS)
    roff = kr - r[:, :, None] + NA_ROWS - 1
    ok = row_ok & (np.abs(e) <= NA_COLS - 1)[None, None, :]
    n_c = 2 * NA_COLS - 1
    idx = np.where(ok, roff * n_c + (e + NA_COLS - 1)[None, None, :], 0)
    flat = rpb.reshape(-1, rpb.shape[-2] * rpb.shape[-1])
    vals = jnp.take(flat, jnp.asarray(idx.reshape(-1), jnp.int32), axis=1)
    vals = vals.reshape(flat.shape[0], 3, q_rows, 1, width)
    tab = jnp.where(jnp.asarray(ok)[None, :, :, None, :], vals * LOG2E, NEG_INF)
    col = np.arange(GRID_W)
    cs = np.clip(col - NA_COLS // 2, 0, GRID_W - NA_COLS)
    col_ok = (col[None, :] >= cs[:, None]) & (col[None, :] < cs[:, None] + NA_COLS)
    cmask = np.where(np.tile(col_ok, (1, k_rows)), 0.0, NEG_INF).astype(np.float32)
    return tab, jnp.asarray(cmask)


def _na_attention(qkv, tab, cmask, layer, cfg, bsz, seq):
    grid_rows = seq // GRID_W
    q_rows, k_rows = tab.shape[2], cmask.shape[1] // GRID_W
    tq, tkw = q_rows * GRID_W, k_rows * GRID_W
    n_blk = grid_rows // q_rows
    assert n_blk >= 3
    heads = cfg["heads_na"]
    k0, v0 = cfg["k0"], cfg["v0"]

    def btype(i):
        return jnp.where(i == 0, 0, jnp.where(i == n_blk - 1, 2, 1))

    kern = functools.partial(_na_kernel, grid_rows=grid_rows, q_rows=q_rows, k_rows=k_rows,
                             n_blk=n_blk)
    return pl.pallas_call(
        kern,
        out_shape=jax.ShapeDtypeStruct((bsz * seq, heads * HEAD_DIM), BF16),
        grid=(bsz, heads, n_blk),
        in_specs=[
            pl.BlockSpec((tq, HEAD_DIM), lambda b, h, i: (b * n_blk + i, h)),
            pl.BlockSpec((seq, HEAD_DIM), lambda b, h, i: (b, k0 + h)),
            pl.BlockSpec((seq, HEAD_DIM), lambda b, h, i: (b, v0 + h)),
            pl.BlockSpec((1, 1, q_rows, 1, tkw),
                         lambda b, h, i: (layer * heads + h, btype(i), 0, 0, 0)),
            pl.BlockSpec((GRID_W, tkw), lambda b, h, i: (0, 0)),
        ],
        out_specs=pl.BlockSpec((tq, HEAD_DIM), lambda b, h, i: (b * n_blk + i, h)),
        scratch_shapes=[pltpu.VMEM((tq, tkw), F32)],
        compiler_params=_params(
            ("parallel", "parallel", "arbitrary"),
            2 * (2 * seq * HEAD_DIM * 2 + 2 * tq * HEAD_DIM * 2) + 6 * tq * tkw * 4),
        name="na_attn",
    )(qkv, qkv, qkv, tab, cmask)


def _t5_bucket_np(rel):
    nb = T5_BUCKETS // 2
    max_exact = nb // 2
    ret = np.where(rel > 0, nb, 0)
    n = np.abs(rel)
    nf = np.maximum(n, 1).astype(np.float64)
    large = max_exact + (np.log(nf / max_exact) / math.log(T5_MAX_DIST / max_exact)
                         * (nb - max_exact)).astype(np.int64)
    large = np.minimum(large, nb - 1)
    return ret + np.where(n < max_exact, n, large)


def _dil_tables(t5_table, tq, n_side):
    n_chunks = 2 * n_side + 1
    rel = ((np.arange(n_chunks) - n_side + 1)[:, None] * tq - np.arange(2 * tq)[None, :]).reshape(-1)
    count = np.zeros(rel.shape, np.int64)
    for window, d in DIL_PATTERNS:
        half = window // (2 * d)
        count += ((rel % d) == 0) & (np.abs(rel) <= half * d)
    bucket = _t5_bucket_np(rel)
    logc = np.log(np.maximum(count, 1)).astype(np.float32)
    f = (t5_table.T[:, bucket] + jnp.asarray(logc)[None, :]) * LOG2E
    f = jnp.where(jnp.asarray(count > 0)[None, :], f, NEG_INF)
    return f.reshape(t5_table.shape[1], n_chunks, 1, 2 * tq)


def _dil_kernel(q_ref, k_ref, v_ref, w_ref, o_ref, bias_ref, *, tq, n_side, n_blk):
    i = pl.program_id(2)

    @pl.when(i == 0)
    def _():
        for c in range(2 * n_side + 1):
            bias_ref[c] = _toeplitz_rows(w_ref[0, c], tq)[:, tq:2 * tq]

    q = q_ref[...]

    def start_of(c):
        blk = jnp.clip(i + c, 0, n_blk - 1)
        return pl.multiple_of(blk * tq, tq)

    def scores_t(c):
        return lax.dot_general(k_ref[pl.ds(start_of(c), tq), :], q, (((1,), (1,)), ((), ())),
                               preferred_element_type=F32) + bias_ref[c + n_side]

    def pv_t(c, pt):
        return lax.dot_general(v_ref[pl.ds(start_of(c), tq), :], pt.astype(BF16),
                               (((0,), (0,)), ((), ())), preferred_element_type=F32)

    order = [0] + [c for c in range(-n_side, n_side + 1) if c != 0]
    st = scores_t(0)
    m = l = acc = None
    for n, c in enumerate(order):
        st_next = scores_t(order[n + 1]) if n + 1 < len(order) else None
        m_cur = jnp.max(st, axis=0, keepdims=True)
        if n == 0:
            m = m_cur
            pt = jnp.exp2(st - m)
            l = jnp.sum(pt, axis=0, keepdims=True)
            acc = pv_t(c, pt)
        else:
            off = jnp.where((i + c >= 0) & (i + c < n_blk), 0.0, NEG_INF).astype(F32)
            m_new = jnp.maximum(m, m_cur + off)
            alpha = jnp.exp2(m - m_new)
            pt = jnp.exp2(st - (m_new - off))
            l = alpha * l + jnp.sum(pt, axis=0, keepdims=True)
            acc = alpha * acc + pv_t(c, pt)
            m = m_new
        st = st_next
    o_ref[...] = (acc / l).T.astype(o_ref.dtype)


def _dil_attention(qkv, tab, cfg, bsz, seq):
    tq = tab.shape[-1] // 2
    n_chunks = tab.shape[1]
    n_side = (n_chunks - 1) // 2
    n_blk = seq // tq
    heads = cfg["heads_dil"]
    q0 = cfg["heads_na"]
    k0, v0 = cfg["k0"] + cfg["heads_na"], cfg["v0"] + cfg["heads_na"]
    kern = functools.partial(_dil_kernel, tq=tq, n_side=n_side, n_blk=n_blk)
    return pl.pallas_call(
        kern,
        out_shape=jax.ShapeDtypeStruct((bsz * seq, heads * HEAD_DIM), BF16),
        grid=(bsz, heads, n_blk),
        in_specs=[
            pl.BlockSpec((tq, HEAD_DIM), lambda b, h, i: (b * n_blk + i, q0 + h)),
            pl.BlockSpec((seq, HEAD_DIM), lambda b, h, i: (b, k0 + h)),
            pl.BlockSpec((seq, HEAD_DIM), lambda b, h, i: (b, v0 + h)),
            pl.BlockSpec((1, n_chunks, 1, 2 * tq), lambda b, h, i: (h, 0, 0, 0)),
        ],
        out_specs=pl.BlockSpec((tq, HEAD_DIM), lambda b, h, i: (b * n_blk + i, h)),
        scratch_shapes=[pltpu.VMEM((n_chunks, tq, tq), F32)],
        compiler_params=_params(
            ("parallel", "parallel", "arbitrary"),
            2 * (2 * seq * HEAD_DIM * 2 + 2 * tq * HEAD_DIM * 2) + n_chunks * tq * tq * 4
            + 2 * tq * 2 * tq * 4 + 6 * tq * tq * 4),
        name="dil_attn",
    )(qkv, qkv, qkv, tab)


def _gqa_kernel(q0_ref, q1_ref, q2_ref, k_ref, v_ref, o_ref, acc_ref, sa_ref, sb_ref, sc_ref, *,
                tq, tk, seq):
    q = jnp.concatenate([q0_ref[...], q1_ref[...], q2_ref[...]], axis=0)
    mq = 3 * tq
    n = seq // tk
    acc_ref[...] = jnp.zeros_like(acc_ref)
    bufs = (sa_ref, sb_ref, sc_ref)
    ahead = len(bufs) - 1

    def scores(j):
        return lax.dot_general(k_ref[j * tk:(j + 1) * tk, :], q, (((1,), (1,)), ((), ())),
                               preferred_element_type=F32)

    for j in range(min(ahead, n)):
        bufs[j % len(bufs)][...] = scores(j)
    m = jnp.full((1, mq), NEG_INF, F32)
    l = jnp.zeros((1, mq), F32)
    for j in range(n):
        if j + ahead < n:
            bufs[(j + ahead) % len(bufs)][...] = scores(j + ahead)
        st = bufs[j % len(bufs)][...]
        m_new = jnp.maximum(m, jnp.max(st, axis=0, keepdims=True))
        alpha = jnp.exp2(m - m_new)
        pt = jnp.exp2(st - m_new)
        l = alpha * l + jnp.sum(pt, axis=0, keepdims=True)
        pv = lax.dot_general(v_ref[j * tk:(j + 1) * tk, :], pt.astype(BF16),
                             (((0,), (0,)), ((), ())), preferred_element_type=F32)
        acc_ref[...] = alpha * acc_ref[...] + pv
        m = m_new
    o = (acc_ref[...] / l).T
    for g in range(3):
        o_ref[:, g * HEAD_DIM:(g + 1) * HEAD_DIM] = o[g * tq:(g + 1) * tq].astype(o_ref.dtype)


def _gqa_attention(qkv, cfg, bsz, seq):
    tq, tk = 256, 512
    assert seq % tk == 0
    n_blk = seq // tq
    group = cfg["gqa_group"]
    assert group == 3
    kv_heads = cfg["kv_heads_gqa"]
    heads = kv_heads * group
    q0 = cfg["q_gqa0"]
    k0, v0 = cfg["k_gqa0"], cfg["v_gqa0"]
    kern = functools.partial(_gqa_kernel, tq=tq, tk=tk, seq=seq)
    q_spec = lambda g: pl.BlockSpec(
        (tq, HEAD_DIM), lambda b, kh, i: (b * n_blk + i, q0 + kh * group + g))
    return pl.pallas_call(
        kern,
        out_shape=jax.ShapeDtypeStruct((bsz * seq, heads * HEAD_DIM), BF16),
        grid=(bsz, kv_heads, n_blk),
        in_specs=[q_spec(0), q_spec(1), q_spec(2),
                  pl.BlockSpec((seq, HEAD_DIM), lambda b, kh, i: (b, k0 + kh)),
                  pl.BlockSpec((seq, HEAD_DIM), lambda b, kh, i: (b, v0 + kh))],
        out_specs=pl.BlockSpec((tq, group * HEAD_DIM), lambda b, kh, i: (b * n_blk + i, kh)),
        scratch_shapes=[pltpu.VMEM((HEAD_DIM, group * tq), F32)]
        + [pltpu.VMEM((tk, group * tq), F32)] * 3,
        compiler_params=_params(
            ("parallel", "parallel", "arbitrary"),
            2 * (2 * seq * HEAD_DIM * 2 + 12 * tq * HEAD_DIM * 2) + 9 * group * tq * tk * 4),
        name="gqa_attn",
    )(qkv, qkv, qkv, qkv, qkv)


def _rope_tables(seq):
    half = HEAD_DIM // 2
    inv = np.exp(-math.log(ROPE_THETA) * np.arange(0, half, 2, dtype=np.float64) / half)
    t = np.arange(seq)
    ang_r = (t // GRID_W)[:, None] * inv[None, :]
    ang_c = (t % GRID_W)[:, None] * inv[None, :]
    cos = np.concatenate([np.cos(ang_r)] * 2 + [np.cos(ang_c)] * 2, axis=-1)
    sin = np.concatenate([-np.sin(ang_r), np.sin(ang_r), -np.sin(ang_c), np.sin(ang_c)], axis=-1)
    return jnp.asarray(cos, F32), jnp.asarray(sin, F32)


def _head_config(d_model, na_rpb, t5_table):
    n_heads = d_model // HEAD_DIM
    heads_na = na_rpb.shape[1]
    heads_dil = t5_table.shape[1]
    heads_gqa = n_heads - heads_na - heads_dil
    kv_heads_gqa = heads_gqa // 3
    k0 = n_heads
    v0 = k0 + heads_na + heads_dil + kv_heads_gqa
    return dict(
        n_heads=n_heads, heads_na=heads_na, heads_dil=heads_dil, gqa_group=3,
        kv_heads_gqa=kv_heads_gqa, q_gqa0=heads_na + heads_dil,
        k0=k0, k_gqa0=k0 + heads_na + heads_dil, v0=v0, v_gqa0=v0 + heads_na + heads_dil)


def kernel(x, c, ada_w, ada_b, ada_layer_emb, norm_gains, w_in, w_out, q_gain, k_gain, na_rpb,
           t5_table, w_mlp_in, w_mlp_out):
    bsz, seq, d = x.shape
    depth = w_in.shape[0]
    cfg = _head_config(d, na_rpb, t5_table)
    grid_rows = seq // GRID_W

    mod = _ada_mod(c, ada_w, ada_b).reshape(bsz, N_MOD, d)
    mods = [mod + ada_layer_emb[l][None] for l in range(depth)]
    cos_t, sin_t = _rope_tables(seq)
    dil_tab = _dil_tables(t5_table, tq=512, n_side=2)
    na_tab, na_cmask = _na_tables(na_rpb, grid_rows, 8, 16)
    stacks = (w_in, w_out, w_mlp_in, w_mlp_out)
    w_cur = [s[0:1].astype(BF16) for s in stacks]

    x2 = x.reshape(bsz * seq, d)
    h = _norm_mod(x2, norm_gains[0, 0], mods[0], 0, 1, seq)
    for l in range(depth):
        m = mods[l]
        qkv = _qkv_proj(h, w_cur[0], 0, cos_t, sin_t, q_gain[l], k_gain[l], cfg, seq)
        o_na = _na_attention(qkv, na_tab, na_cmask, l, cfg, bsz, seq)
        o_dil = _dil_attention(qkv, dil_tab, cfg, bsz, seq)
        o_gqa = _gqa_attention(qkv, cfg, bsz, seq)
        y = _out_proj(o_na, o_dil, o_gqa, w_cur[1], 0)
        x2, h = _resid_norm_mod(x2, y, norm_gains[l, 1], norm_gains[l, 2], m, m, 2, 3, 4, seq)
        u, w_next = _mlp_in(h, w_cur[2], 0, stacks if l + 1 < depth else (), l + 1)
        y = _mlp_out(u, w_cur[3], 0)
        w_cur = w_next
        if l + 1 < depth:
            x2, h = _resid_norm_mod(x2, y, norm_gains[l, 3], norm_gains[l + 1, 0], m, mods[l + 1],
                                    5, 0, 1, seq)
        else:
            x2 = _resid_norm(x2, y, norm_gains[l, 3], m, 5, seq)
    return x2.reshape(bsz, seq, d)
```

```python
import functools
import math

import numpy as np
import jax
import jax.numpy as jnp
from jax import lax
from jax.experimental import pallas as pl
from jax.experimental.pallas import tpu as pltpu

HEAD_DIM = 128
GRID_W = 64
NA_ROWS = 8
NA_COLS = 16
DIL_PATTERNS = ((128, 1), (512, 4), (2048, 16))
T5_BUCKETS = 32
T5_MAX_DIST = 1024
ROPE_THETA = 10000.0
N_MOD = 6
EPS = 1e-6
NEG_INF = -1e30
ATTN_SCALE = HEAD_DIM ** -0.5
LOG2E = math.log2(math.e)

V7X_VMEM_BYTES = 64 * 1024 * 1024
V7X_LANES = 128

BF16 = jnp.bfloat16
F32 = jnp.float32


def _params(semantics, vmem_bytes):
    limit = min(int(vmem_bytes * 1.25) + (4 << 20), V7X_VMEM_BYTES - (6 << 20))
    return pltpu.CompilerParams(dimension_semantics=semantics, vmem_limit_bytes=limit)


def _ada_kernel(c_ref, w_ref, b_ref, o_ref):
    c = c_ref[...]
    a = c * jax.nn.sigmoid(c)
    a_hi = a.astype(BF16)
    a_lo = (a - a_hi.astype(F32)).astype(BF16)
    w = w_ref[...]
    w_hi = w.astype(BF16)
    w_lo = (w - w_hi.astype(F32)).astype(BF16)
    acc = jnp.dot(a_hi, w_hi, preferred_element_type=F32)
    acc += jnp.dot(a_hi, w_lo, preferred_element_type=F32)
    acc += jnp.dot(a_lo, w_hi, preferred_element_type=F32)
    o_ref[...] = acc + b_ref[...]


def _ada_mod(c, ada_w, ada_b):
    bsz, d = c.shape
    n = ada_w.shape[1]
    rows = 8
    tn = 512
    c_pad = jnp.zeros((rows, d), F32).at[:bsz].set(c)
    out = pl.pallas_call(
        _ada_kernel,
        out_shape=jax.ShapeDtypeStruct((rows, n), F32),
        grid=(n // tn,),
        in_specs=[
            pl.BlockSpec((rows, d), lambda j: (0, 0)),
            pl.BlockSpec((d, tn), lambda j: (0, j)),
            pl.BlockSpec((1, tn), lambda j: (0, j)),
        ],
        out_specs=pl.BlockSpec((rows, tn), lambda j: (0, j)),
        compiler_params=_params(("arbitrary",), 2 * d * tn * 4 + 3 * d * tn * 2),
        name="ada_mod",
    )(c_pad, ada_w, ada_b.reshape(1, n))
    return out[:bsz]


def _norm_mod_kernel(x_ref, g_ref, m_ref, o_ref, *, shift_row, scale_row):
    x = x_ref[...]
    y = x * lax.rsqrt(jnp.mean(x * x, axis=-1, keepdims=True) + EPS) * g_ref[...]
    shift = m_ref[0, shift_row:shift_row + 1, :]
    scale = m_ref[0, scale_row:scale_row + 1, :]
    o_ref[...] = (y * (1.0 + scale) + shift).astype(o_ref.dtype)


def _norm_mod(x2, gain, m, shift_row, scale_row, seq):
    rows, d = x2.shape
    tr = 256
    per_b = seq // tr
    return pl.pallas_call(
        functools.partial(_norm_mod_kernel, shift_row=shift_row, scale_row=scale_row),
        out_shape=jax.ShapeDtypeStruct((rows, d), BF16),
        grid=(rows // tr,),
        in_specs=[
            pl.BlockSpec((tr, d), lambda i: (i, 0)),
            pl.BlockSpec((1, d), lambda i: (0, 0)),
            pl.BlockSpec((1, N_MOD, d), lambda i: (i // per_b, 0, 0)),
        ],
        out_specs=pl.BlockSpec((tr, d), lambda i: (i, 0)),
        compiler_params=_params(("parallel",), 2 * tr * d * 6 + 4 * tr * d * 4),
        name="norm_mod",
    )(x2, gain.reshape(1, d), m)


def _resid_norm_kernel(x_ref, y_ref, g_ref, m_ref, o_ref, *, gate_row):
    y = y_ref[...]
    yn = y * lax.rsqrt(jnp.mean(y * y, axis=-1, keepdims=True) + EPS) * g_ref[...]
    gate = m_ref[0, gate_row:gate_row + 1, :]
    o_ref[...] = x_ref[...] + gate * yn


def _resid_norm(x2, y2, gain, m, gate_row, seq):
    rows, d = x2.shape
    tr = 256
    per_b = seq // tr
    return pl.pallas_call(
        functools.partial(_resid_norm_kernel, gate_row=gate_row),
        out_shape=jax.ShapeDtypeStruct((rows, d), F32),
        grid=(rows // tr,),
        in_specs=[
            pl.BlockSpec((tr, d), lambda i: (i, 0)),
            pl.BlockSpec((tr, d), lambda i: (i, 0)),
            pl.BlockSpec((1, d), lambda i: (0, 0)),
            pl.BlockSpec((1, N_MOD, d), lambda i: (i // per_b, 0, 0)),
        ],
        out_specs=pl.BlockSpec((tr, d), lambda i: (i, 0)),
        compiler_params=_params(("parallel",), 2 * tr * d * 12 + 4 * tr * d * 4),
        name="resid_norm",
    )(x2, y2, gain.reshape(1, d), m)


def _resid_norm_mod_kernel(x_ref, y_ref, g1_ref, g2_ref, ma_ref, mb_ref, xo_ref, ho_ref, *,
                           gate_row, shift_row, scale_row):
    y = y_ref[...]
    yn = y * lax.rsqrt(jnp.mean(y * y, axis=-1, keepdims=True) + EPS) * g1_ref[...]
    x = x_ref[...] + ma_ref[0, gate_row:gate_row + 1, :] * yn
    xo_ref[...] = x
    h = x * lax.rsqrt(jnp.mean(x * x, axis=-1, keepdims=True) + EPS) * g2_ref[...]
    shift = mb_ref[0, shift_row:shift_row + 1, :]
    scale = mb_ref[0, scale_row:scale_row + 1, :]
    ho_ref[...] = (h * (1.0 + scale) + shift).astype(ho_ref.dtype)


def _resid_norm_mod(x2, y2, gain1, gain2, m_gate, m_mod, gate_row, shift_row, scale_row, seq):
    rows, d = x2.shape
    tr = 256
    per_b = seq // tr
    row_spec = pl.BlockSpec((tr, d), lambda i: (i, 0))
    vec_spec = pl.BlockSpec((1, d), lambda i: (0, 0))
    mod_spec = pl.BlockSpec((1, N_MOD, d), lambda i: (i // per_b, 0, 0))
    return pl.pallas_call(
        functools.partial(_resid_norm_mod_kernel, gate_row=gate_row, shift_row=shift_row,
                          scale_row=scale_row),
        out_shape=(jax.ShapeDtypeStruct((rows, d), F32), jax.ShapeDtypeStruct((rows, d), BF16)),
        grid=(rows // tr,),
        in_specs=[row_spec, row_spec, vec_spec, vec_spec, mod_spec, mod_spec],
        out_specs=(row_spec, row_spec),
        compiler_params=_params(("parallel",), 2 * tr * d * 14 + 5 * tr * d * 4),
        name="resid_norm_mod",
    )(x2, y2, gain1.reshape(1, d), gain2.reshape(1, d), m_gate, m_mod)


def _swap_half_pairs(y):
    lane = lax.broadcasted_iota(jnp.int32, y.shape, 1)
    fwd = pltpu.roll(y, HEAD_DIM - 32, axis=1)
    bwd = pltpu.roll(y, 32, axis=1)
    return jnp.where((lane % 64) < 32, fwd, bwd)


def _qkv_kernel(a_ref, w_ref, cos_ref, sin_ref, qg_ref, kg_ref, o_ref, *,
                n_q_blocks, gq_lo, gq_hi, gk_lo, gk_hi, heads_per_block):
    j = pl.program_id(1)
    acc = jnp.dot(a_ref[...], w_ref[...], preferred_element_type=F32)
    is_q = j < n_q_blocks
    is_gq = (j >= gq_lo) & (j < gq_hi)
    is_gk = (j >= gk_lo) & (j < gk_hi)
    rope = is_gq | is_gk
    out_scale = jnp.where(is_q, ATTN_SCALE * LOG2E, 1.0).astype(F32)

    @pl.when(rope)
    def _():
        gain = jnp.where(is_gq, qg_ref[...], kg_ref[...])
        cos = cos_ref[...]
        sin = sin_ref[...]
        for h in range(heads_per_block):
            xh = acc[:, h * HEAD_DIM:(h + 1) * HEAD_DIM]
            y = xh * lax.rsqrt(jnp.mean(xh * xh, axis=-1, keepdims=True) + EPS) * gain
            y = y * cos + _swap_half_pairs(y) * sin
            o_ref[:, h * HEAD_DIM:(h + 1) * HEAD_DIM] = (y * out_scale).astype(o_ref.dtype)

    @pl.when(jnp.logical_not(rope))
    def _():
        o_ref[...] = (acc * out_scale).astype(o_ref.dtype)


def _qkv_proj(h2, w, layer, cos_t, sin_t, q_gain, k_gain, cfg, seq):
    m, k = h2.shape
    n = w.shape[2]
    tm, tn = 1024, 512
    hpb = tn // HEAD_DIM
    blk = lambda head: head // hpb
    kern = functools.partial(
        _qkv_kernel,
        n_q_blocks=blk(cfg["n_heads"]),
        gq_lo=blk(cfg["q_gqa0"]), gq_hi=blk(cfg["n_heads"]),
        gk_lo=blk(cfg["k_gqa0"]), gk_hi=blk(cfg["v0"]),
        heads_per_block=hpb)
    per_b = seq // tm
    return pl.pallas_call(
        kern,
        out_shape=jax.ShapeDtypeStruct((m, n), BF16),
        grid=(m // tm, n // tn),
        in_specs=[
            pl.BlockSpec((tm, k), lambda i, j: (i, 0)),
            pl.BlockSpec((None, k, tn), lambda i, j: (layer, 0, j)),
            pl.BlockSpec((tm, HEAD_DIM), lambda i, j: (i % per_b, 0)),
            pl.BlockSpec((tm, HEAD_DIM), lambda i, j: (i % per_b, 0)),
            pl.BlockSpec((1, HEAD_DIM), lambda i, j: (0, 0)),
            pl.BlockSpec((1, HEAD_DIM), lambda i, j: (0, 0)),
        ],
        out_specs=pl.BlockSpec((tm, tn), lambda i, j: (i, j)),
        compiler_params=_params(
            ("parallel", "arbitrary"),
            2 * (tm * k * 2 + k * tn * 2 + tm * tn * 2 + 2 * tm * HEAD_DIM * 4) + 3 * tm * tn * 4),
        name="qkv_proj",
    )(h2, w, cos_t, sin_t, q_gain.reshape(1, HEAD_DIM), k_gain.reshape(1, HEAD_DIM))


def _out_proj_kernel(a0_ref, a1_ref, a2_ref, w_ref, o_ref, *, splits):
    s0, s1, s2 = splits
    acc = jnp.dot(a0_ref[...], w_ref[0:s0, :], preferred_element_type=F32)
    acc += jnp.dot(a1_ref[...], w_ref[s0:s0 + s1, :], preferred_element_type=F32)
    acc += jnp.dot(a2_ref[...], w_ref[s0 + s1:s0 + s1 + s2, :], preferred_element_type=F32)
    o_ref[...] = acc


def _out_proj(o_na, o_dil, o_gqa, w, layer):
    m = o_na.shape[0]
    _, k, n = w.shape
    splits = (o_na.shape[1], o_dil.shape[1], o_gqa.shape[1])
    tm, tn = 1024, 1024
    return pl.pallas_call(
        functools.partial(_out_proj_kernel, splits=splits),
        out_shape=jax.ShapeDtypeStruct((m, n), F32),
        grid=(m // tm, n // tn),
        in_specs=[
            pl.BlockSpec((tm, splits[0]), lambda i, j: (i, 0)),
            pl.BlockSpec((tm, splits[1]), lambda i, j: (i, 0)),
            pl.BlockSpec((tm, splits[2]), lambda i, j: (i, 0)),
            pl.BlockSpec((None, k, tn), lambda i, j: (layer, 0, j)),
        ],
        out_specs=pl.BlockSpec((tm, tn), lambda i, j: (i, j)),
        compiler_params=_params(
            ("parallel", "arbitrary"),
            2 * (tm * k * 2 + k * tn * 2 + tm * tn * 4) + 2 * tm * tn * 4),
        name="out_proj",
    )(o_na, o_dil, o_gqa, w)


def _mlp_in_kernel(a_ref, w_ref, *refs, n_cast):
    src_refs, o_ref, dst_refs = refs[:n_cast], refs[n_cast], refs[n_cast + 1:]
    acc = jnp.dot(a_ref[...], w_ref[...], preferred_element_type=F32)
    r = jnp.maximum(acc, 0.0)
    o_ref[...] = (r * r).astype(o_ref.dtype)
    for src, dst in zip(src_refs, dst_refs):
        dst[...] = src[...].astype(dst.dtype)


def _mlp_in(h2, w, layer, cast_stacks=(), cast_layer=0):
    m, k = h2.shape
    n = w.shape[2]
    tm, tn = 1024, 1024
    grid = (m // tm, n // tn)
    steps = grid[0] * grid[1]
    cast_in, cast_out, cast_shapes, cast_bytes = [], [], [], 0
    for s in cast_stacks:
        _, rows, cols = s.shape
        slab = rows // steps
        assert slab * steps == rows and slab % 16 == 0
        cast_in.append(pl.BlockSpec(
            (None, slab, cols), lambda i, j: (cast_layer, i * grid[1] + j, 0)))
        cast_out.append(pl.BlockSpec((None, slab, cols), lambda i, j: (0, i * grid[1] + j, 0)))
        cast_shapes.append(jax.ShapeDtypeStruct((1, rows, cols), BF16))
        cast_bytes += slab * cols * 6
    outs = pl.pallas_call(
        functools.partial(_mlp_in_kernel, n_cast=len(cast_stacks)),
        out_shape=[jax.ShapeDtypeStruct((m, n), BF16)] + cast_shapes,
        grid=grid,
        in_specs=[
            pl.BlockSpec((tm, k), lambda i, j: (i, 0)),
            pl.BlockSpec((None, k, tn), lambda i, j: (layer, 0, j)),
        ] + cast_in,
        out_specs=[pl.BlockSpec((tm, tn), lambda i, j: (i, j))] + cast_out,
        compiler_params=_params(
            ("arbitrary", "arbitrary"),
            2 * (tm * k * 2 + k * tn * 2 + tm * tn * 2 + cast_bytes) + 2 * tm * tn * 4),
        name="mlp_in",
    )(h2, w, *cast_stacks)
    return outs[0], outs[1:]


def _mlp_out_kernel(a_ref, w_ref, o_ref):
    kk = pl.program_id(2)
    acc = jnp.dot(a_ref[...], w_ref[...], preferred_element_type=F32)

    @pl.when(kk == 0)
    def _():
        o_ref[...] = acc

    @pl.when(kk != 0)
    def _():
        o_ref[...] += acc


def _mlp_out(u2, w, layer):
    m, k = u2.shape
    n = w.shape[2]
    tm, tn, tk = 1024, 1024, 4096
    return pl.pallas_call(
        _mlp_out_kernel,
        out_shape=jax.ShapeDtypeStruct((m, n), F32),
        grid=(m // tm, n // tn, k // tk),
        in_specs=[
            pl.BlockSpec((tm, tk), lambda i, j, kk: (i, kk)),
            pl.BlockSpec((None, tk, tn), lambda i, j, kk: (layer, kk, j)),
        ],
        out_specs=pl.BlockSpec((tm, tn), lambda i, j, kk: (i, j)),
        compiler_params=_params(
            ("parallel", "arbitrary", "arbitrary"),
            2 * (tm * tk * 2 + tk * tn * 2 + tm * tn * 4) + 2 * tm * tn * 4),
        name="mlp_out",
    )(u2, w)


def _toeplitz_rows(row, n_rows):
    x = jnp.broadcast_to(row, (n_rows, row.shape[-1]))
    return pltpu.roll(x, 0, axis=1, stride=1, stride_axis=0)


def _na_kernel(q_ref, k_ref, v_ref, w_ref, cm_ref, o_ref, bias_ref, *, grid_rows, q_rows, k_rows,
               n_blk):
    i = pl.program_id(2)

    @pl.when((i == 0) | (i == 1) | (i == n_blk - 1))
    def _():
        for kr in range(k_rows):
            band = _toeplitz_rows(w_ref[0, 0, kr], GRID_W)
            bias_ref[kr * GRID_W:(kr + 1) * GRID_W, :] = band + cm_ref[...]

    kb = jnp.clip(i * q_rows - NA_ROWS // 2, 0, grid_rows - k_rows) * GRID_W
    kb = pl.multiple_of(kb, 256)
    kw = k_ref[pl.ds(kb, k_rows * GRID_W), :]
    vw = v_ref[pl.ds(kb, k_rows * GRID_W), :]
    st = lax.dot_general(kw, q_ref[...], (((1,), (1,)), ((), ())),
                         preferred_element_type=F32) + bias_ref[...]
    m = jnp.max(st, axis=0, keepdims=True)
    pt = jnp.exp2(st - m)
    l = jnp.sum(pt, axis=0, keepdims=True)
    ot = lax.dot_general(vw, pt.astype(BF16), (((0,), (0,)), ((), ())),
                         preferred_element_type=F32)
    o_ref[...] = (ot / l).T.astype(o_ref.dtype)


def _na_tables(rpb, grid_rows, q_rows, k_rows):
    n_blk = grid_rows // q_rows
    width = q_rows * GRID_W
    blocks = np.array([0, 1, n_blk - 1])
    j = np.arange(width)
    a = ((j + GRID_W // 2) // GRID_W) % q_rows
    e = (j + GRID_W // 2) % GRID_W - GRID_W // 2
    r = blocks[:, None] * q_rows + a[None, :]
    kb = np.clip(blocks * q_rows - NA_ROWS // 2, 0, grid_rows - k_rows)
    rs = np.clip(r - NA_ROWS // 2, 0, grid_rows - NA_ROWS)
    kr = kb[:, None, None] + np.arange(k_rows)[None, :, None]
    row_ok = (kr >= rs[:, None, :]) & (kr < rs[:, None, :] + NA_ROWS)
    roff = kr - r[:, None, :] + NA_ROWS - 1
    ok = row_ok & (np.abs(e) <= NA_COLS - 1)[None, None, :]
    n_c = 2 * NA_COLS - 1
    idx = np.where(ok, roff * n_c + (NA_COLS - 1 - e)[None, None, :], 0)
    flat = rpb.reshape(-1, rpb.shape[-2] * rpb.shape[-1])
    vals = jnp.take(flat, jnp.asarray(idx.reshape(-1), jnp.int32), axis=1)
    vals = vals.reshape(flat.shape[0], 3, k_rows, 1, width)
    tab = jnp.where(jnp.asarray(ok)[None, :, :, None, :], vals * LOG2E, NEG_INF)
    col = np.arange(GRID_W)
    cs = np.clip(col - NA_COLS // 2, 0, GRID_W - NA_COLS)
    col_ok = (col[None, :] >= cs[:, None]) & (col[None, :] < cs[:, None] + NA_COLS)
    cmask = np.where(np.tile(col_ok.T, (1, q_rows)), 0.0, NEG_INF).astype(np.float32)
    return tab, jnp.asarray(cmask)


def _na_attention(qkv, tab, cmask, layer, cfg, bsz, seq):
    grid_rows = seq // GRID_W
    k_rows, q_rows = tab.shape[2], cmask.shape[1] // GRID_W
    tq, tkw = q_rows * GRID_W, k_rows * GRID_W
    n_blk = grid_rows // q_rows
    assert n_blk >= 3
    heads = cfg["heads_na"]
    k0, v0 = cfg["k0"], cfg["v0"]

    def btype(i):
        return jnp.where(i == 0, 0, jnp.where(i == n_blk - 1, 2, 1))

    kern = functools.partial(_na_kernel, grid_rows=grid_rows, q_rows=q_rows, k_rows=k_rows,
                             n_blk=n_blk)
    return pl.pallas_call(
        kern,
        out_shape=jax.ShapeDtypeStruct((bsz * seq, heads * HEAD_DIM), BF16),
        grid=(bsz, heads, n_blk),
        in_specs=[
            pl.BlockSpec((tq, HEAD_DIM), lambda b, h, i: (b * n_blk + i, h)),
            pl.BlockSpec((seq, HEAD_DIM), lambda b, h, i: (b, k0 + h)),
            pl.BlockSpec((seq, HEAD_DIM), lambda b, h, i: (b, v0 + h)),
            pl.BlockSpec((1, 1, k_rows, 1, tq),
                         lambda b, h, i: (layer * heads + h, btype(i), 0, 0, 0)),
            pl.BlockSpec((GRID_W, tq), lambda b, h, i: (0, 0)),
        ],
        out_specs=pl.BlockSpec((tq, HEAD_DIM), lambda b, h, i: (b * n_blk + i, h)),
        scratch_shapes=[pltpu.VMEM((tkw, tq), F32)],
        compiler_params=_params(
            ("parallel", "parallel", "arbitrary"),
            2 * (2 * seq * HEAD_DIM * 2 + 2 * tq * HEAD_DIM * 2) + 6 * tq * tkw * 4),
        name="na_attn",
    )(qkv, qkv, qkv, tab, cmask)


def _t5_bucket_np(rel):
    nb = T5_BUCKETS // 2
    max_exact = nb // 2
    ret = np.where(rel > 0, nb, 0)
    n = np.abs(rel)
    nf = np.maximum(n, 1).astype(np.float64)
    large = max_exact + (np.log(nf / max_exact) / math.log(T5_MAX_DIST / max_exact)
                         * (nb - max_exact)).astype(np.int64)
    large = np.minimum(large, nb - 1)
    return ret + np.where(n < max_exact, n, large)


def _dil_tables(t5_table, tq, n_side):
    n_chunks = 2 * n_side + 1
    rel = ((np.arange(n_chunks) - n_side + 1)[:, None] * tq - np.arange(2 * tq)[None, :]).reshape(-1)
    count = np.zeros(rel.shape, np.int64)
    for window, d in DIL_PATTERNS:
        half = window // (2 * d)
        count += ((rel % d) == 0) & (np.abs(rel) <= half * d)
    bucket = _t5_bucket_np(rel)
    logc = np.log(np.maximum(count, 1)).astype(np.float32)
    f = (t5_table.T[:, bucket] + jnp.asarray(logc)[None, :]) * LOG2E
    f = jnp.where(jnp.asarray(count > 0)[None, :], f, NEG_INF)
    return f.reshape(t5_table.shape[1], n_chunks, 1, 2 * tq)


def _dil_kernel(q_ref, k_ref, v_ref, w_ref, o_ref, bias_ref, *, tq, n_side, n_blk):
    i = pl.program_id(2)

    @pl.when(i == 0)
    def _():
        for c in range(2 * n_side + 1):
            bias_ref[c] = _toeplitz_rows(w_ref[0, c], tq)[:, tq:2 * tq]

    q = q_ref[...]

    def start_of(c):
        blk = jnp.clip(i + c, 0, n_blk - 1)
        return pl.multiple_of(blk * tq, tq)

    def scores_t(c):
        return lax.dot_general(k_ref[pl.ds(start_of(c), tq), :], q, (((1,), (1,)), ((), ())),
                               preferred_element_type=F32) + bias_ref[c + n_side]

    def pv_t(c, pt):
        return lax.dot_general(v_ref[pl.ds(start_of(c), tq), :], pt.astype(BF16),
                               (((0,), (0,)), ((), ())), preferred_element_type=F32)

    order = [0] + [c for c in range(-n_side, n_side + 1) if c != 0]
    st = scores_t(0)
    m = l = acc = None
    for n, c in enumerate(order):
        st_next = scores_t(order[n + 1]) if n + 1 < len(order) else None
        m_cur = jnp.max(st, axis=0, keepdims=True)
        if n == 0:
            m = m_cur
            pt = jnp.exp2(st - m)
            l = jnp.sum(pt, axis=0, keepdims=True)
            acc = pv_t(c, pt)
        else:
            off = jnp.where((i + c >= 0) & (i + c < n_blk), 0.0, NEG_INF).astype(F32)
            m_new = jnp.maximum(m, m_cur + off)
            alpha = jnp.exp2(m - m_new)
            pt = jnp.exp2(st - (m_new - off))
            l = alpha * l + jnp.sum(pt, axis=0, keepdims=True)
            acc = alpha * acc + pv_t(c, pt)
            m = m_new
        st = st_next
    o_ref[...] = (acc / l).T.astype(o_ref.dtype)


def _dil_attention(qkv, tab, cfg, bsz, seq):
    tq = tab.shape[-1] // 2
    n_chunks = tab.shape[1]
    n_side = (n_chunks - 1) // 2
    n_blk = seq // tq
    heads = cfg["heads_dil"]
    q0 = cfg["heads_na"]
    k0, v0 = cfg["k0"] + cfg["heads_na"], cfg["v0"] + cfg["heads_na"]
    kern = functools.partial(_dil_kernel, tq=tq, n_side=n_side, n_blk=n_blk)
    return pl.pallas_call(
        kern,
        out_shape=jax.ShapeDtypeStruct((bsz * seq, heads * HEAD_DIM), BF16),
        grid=(bsz, heads, n_blk),
        in_specs=[
            pl.BlockSpec((tq, HEAD_DIM), lambda b, h, i: (b * n_blk + i, q0 + h)),
            pl.BlockSpec((seq, HEAD_DIM), lambda b, h, i: (b, k0 + h)),
            pl.BlockSpec((seq, HEAD_DIM), lambda b, h, i: (b, v0 + h)),
            pl.BlockSpec((1, n_chunks, 1, 2 * tq), lambda b, h, i: (h, 0, 0, 0)),
        ],
        out_specs=pl.BlockSpec((tq, HEAD_DIM), lambda b, h, i: (b * n_blk + i, h)),
        scratch_shapes=[pltpu.VMEM((n_chunks, tq, tq), F32)],
        compiler_params=_params(
            ("parallel", "parallel", "arbitrary"),
            2 * (2 * seq * HEAD_DIM * 2 + 2 * tq * HEAD_DIM * 2) + n_chunks * tq * tq * 4
            + 2 * tq * 2 * tq * 4 + 6 * tq * tq * 4),
        name="dil_attn",
    )(qkv, qkv, qkv, tab)


def _gqa_kernel(q0_ref, q1_ref, q2_ref, k_ref, v_ref, o_ref, acc_ref, sa_ref, sb_ref, sc_ref, *,
                tq, tk, seq):
    q = jnp.concatenate([q0_ref[...], q1_ref[...], q2_ref[...]], axis=0)
    mq = 3 * tq
    n = seq // tk
    acc_ref[...] = jnp.zeros_like(acc_ref)
    bufs = (sa_ref, sb_ref, sc_ref)
    ahead = len(bufs) - 1

    def scores(j):
        return lax.dot_general(k_ref[j * tk:(j + 1) * tk, :], q, (((1,), (1,)), ((), ())),
                               preferred_element_type=F32)

    for j in range(min(ahead, n)):
        bufs[j % len(bufs)][...] = scores(j)
    m = jnp.full((1, mq), NEG_INF, F32)
    l = jnp.zeros((1, mq), F32)
    for j in range(n):
        if j + ahead < n:
            bufs[(j + ahead) % len(bufs)][...] = scores(j + ahead)
        st = bufs[j % len(bufs)][...]
        m_new = jnp.maximum(m, jnp.max(st, axis=0, keepdims=True))
        alpha = jnp.exp2(m - m_new)
        pt = jnp.exp2(st - m_new)
        l = alpha * l + jnp.sum(pt, axis=0, keepdims=True)
        pv = lax.dot_general(v_ref[j * tk:(j + 1) * tk, :], pt.astype(BF16),
                             (((0,), (0,)), ((), ())), preferred_element_type=F32)
        acc_ref[...] = alpha * acc_ref[...] + pv
        m = m_new
    o = (acc_ref[...] / l).T
    for g in range(3):
        o_ref[:, g * HEAD_DIM:(g + 1) * HEAD_DIM] = o[g * tq:(g + 1) * tq].astype(o_ref.dtype)


def _gqa_attention(qkv, cfg, bsz, seq):
    tq, tk = 256, 512
    assert seq % tk == 0
    n_blk = seq // tq
    group = cfg["gqa_group"]
    assert group == 3
    kv_heads = cfg["kv_heads_gqa"]
    heads = kv_heads * group
    q0 = cfg["q_gqa0"]
    k0, v0 = cfg["k_gqa0"], cfg["v_gqa0"]
    kern = functools.partial(_gqa_kernel, tq=tq, tk=tk, seq=seq)
    q_spec = lambda g: pl.BlockSpec(
        (tq, HEAD_DIM), lambda b, kh, i: (b * n_blk + i, q0 + kh * group + g))
    return pl.pallas_call(
        kern,
        out_shape=jax.ShapeDtypeStruct((bsz * seq, heads * HEAD_DIM), BF16),
        grid=(bsz, kv_heads, n_blk),
        in_specs=[q_spec(0), q_spec(1), q_spec(2),
                  pl.BlockSpec((seq, HEAD_DIM), lambda b, kh, i: (b, k0 + kh)),
                  pl.BlockSpec((seq, HEAD_DIM), lambda b, kh, i: (b, v0 + kh))],
        out_specs=pl.BlockSpec((tq, group * HEAD_DIM), lambda b, kh, i: (b * n_blk + i, kh)),
        scratch_shapes=[pltpu.VMEM((HEAD_DIM, group * tq), F32)]
        + [pltpu.VMEM((tk, group * tq), F32)] * 3,
        compiler_params=_params(
            ("parallel", "parallel", "arbitrary"),
            2 * (2 * seq * HEAD_DIM * 2 + 12 * tq * HEAD_DIM * 2) + 9 * group * tq * tk * 4),
        name="gqa_attn",
    )(qkv, qkv, qkv, qkv, qkv)


def _rope_tables(seq):
    half = HEAD_DIM // 2
    inv = np.exp(-math.log(ROPE_THETA) * np.arange(0, half, 2, dtype=np.float64) / half)
    t = np.arange(seq)
    ang_r = (t // GRID_W)[:, None] * inv[None, :]
    ang_c = (t % GRID_W)[:, None] * inv[None, :]
    cos = np.concatenate([np.cos(ang_r)] * 2 + [np.cos(ang_c)] * 2, axis=-1)
    sin = np.concatenate([-np.sin(ang_r), np.sin(ang_r), -np.sin(ang_c), np.sin(ang_c)], axis=-1)
    return jnp.asarray(cos, F32), jnp.asarray(sin, F32)


def _head_config(d_model, na_rpb, t5_table):
    n_heads = d_model // HEAD_DIM
    heads_na = na_rpb.shape[1]
    heads_dil = t5_table.shape[1]
    heads_gqa = n_heads - heads_na - heads_dil
    kv_heads_gqa = heads_gqa // 3
    k0 = n_heads
    v0 = k0 + heads_na + heads_dil + kv_heads_gqa
    return dict(
        n_heads=n_heads, heads_na=heads_na, heads_dil=heads_dil, gqa_group=3,
        kv_heads_gqa=kv_heads_gqa, q_gqa0=heads_na + heads_dil,
        k0=k0, k_gqa0=k0 + heads_na + heads_dil, v0=v0, v_gqa0=v0 + heads_na + heads_dil)


def kernel(x, c, ada_w, ada_b, ada_layer_emb, norm_gains, w_in, w_out, q_gain, k_gain, na_rpb,
           t5_table, w_mlp_in, w_mlp_out):
    bsz, seq, d = x.shape
    depth = w_in.shape[0]
    cfg = _head_config(d, na_rpb, t5_table)
    grid_rows = seq // GRID_W

    mod = _ada_mod(c, ada_w, ada_b).reshape(bsz, N_MOD, d)
    mods = [mod + ada_layer_emb[l][None] for l in range(depth)]
    cos_t, sin_t = _rope_tables(seq)
    dil_tab = _dil_tables(t5_table, tq=512, n_side=2)
    na_tab, na_cmask = _na_tables(na_rpb, grid_rows, 8, 16)
    stacks = (w_in, w_out, w_mlp_in, w_mlp_out)
    w_cur = [s[0:1].astype(BF16) for s in stacks]

    x2 = x.reshape(bsz * seq, d)
    h = _norm_mod(x2, norm_gains[0, 0], mods[0], 0, 1, seq)
    for l in range(depth):
        m = mods[l]
        qkv = _qkv_proj(h, w_cur[0], 0, cos_t, sin_t, q_gain[l], k_gain[l], cfg, seq)
        o_na = _na_attention(qkv, na_tab, na_cmask, l, cfg, bsz, seq)
        o_dil = _dil_attention(qkv, dil_tab, cfg, bsz, seq)
        o_gqa = _gqa_attention(qkv, cfg, bsz, seq)
        y = _out_proj(o_na, o_dil, o_gqa, w_cur[1], 0)
        x2, h = _resid_norm_mod(x2, y, norm_gains[l, 1], norm_gains[l, 2], m, m, 2, 3, 4, seq)
        u, w_next = _mlp_in(h, w_cur[2], 0, stacks if l + 1 < depth else (), l + 1)
        y = _mlp_out(u, w_cur[3], 0)
        w_cur = w_next
        if l + 1 < depth:
            x2, h = _resid_norm_mod(x2, y, norm_gains[l, 3], norm_gains[l + 1, 0], m, mods[l + 1],
                                    5, 0, 1, seq)
        else:
            x2 = _resid_norm(x2, y, norm_gains[l, 3], m, 5, seq)
    return x2.reshape(bsz, seq, d)
```
